```python
import functools
import jax, jax.numpy as jnp
from jax import lax
import numpy as np

D_MODEL = 1024
BATCH = 4
SEQ = 8192
DEPTH = 2
DEC_BATCH = 32
DEC_SEQ = 4
PAST_LEN = 16384
PAGE_SIZE = 128

H_FOX = 8
DH_FOX = 64
H_ML = 4
DH_ML = 128
H_MEM = 4
DH_MEM = 128
N_MEM = 256
D_FF = 2816
CONV_W = 3
Q_BLOCK = 128
ML_CHUNK = 128
FORGET_BIAS = 3.0
EPS = 1e-6
FOX_W = H_FOX * DH_FOX
ML_W = H_ML * DH_ML
MEM_W = H_MEM * DH_MEM

kernel_name = 'fox_mlstm_memory_hybrid_step'

F32 = jnp.float32


def _in_offsets():
    sizes = (('fox_q', FOX_W), ('fox_k', FOX_W), ('fox_v', FOX_W), ('fox_f', H_FOX),
             ('ml_q', ML_W), ('ml_k', ML_W), ('ml_v', ML_W), ('ml_i', H_ML), ('ml_f', H_ML), ('ml_o', ML_W),
             ('mem_q', MEM_W), ('g_fox', D_MODEL), ('g_ml', D_MODEL), ('g_mem', D_MODEL))
    offs, start = {}, 0
    for name, size in sizes:
        offs[name] = (start, size)
        start += size
    return offs, start


def rmsnorm(x, g):
    xf = x.astype(F32)
    return xf * lax.rsqrt(jnp.mean(xf * xf, axis=-1, keepdims=True) + EPS) * g.astype(F32)


def fox_prompt(q, k, v, logf):
    b, s, h, d = q.shape
    q, k, v = q.astype(F32), k.astype(F32), v.astype(F32)
    cum = jnp.cumsum(logf, axis=1).transpose(0, 2, 1)
    kpos = jnp.arange(s)

    def block(i):
        q0 = i * Q_BLOCK
        qb = lax.dynamic_slice_in_dim(q, q0, Q_BLOCK, axis=1)
        cb = lax.dynamic_slice_in_dim(cum, q0, Q_BLOCK, axis=2)
        logits = (jnp.einsum('bqhd,bkhd->bhqk', qb, k) * DH_FOX ** -0.5
                  + cb[..., :, None] - cum[:, :, None, :])
        mask = (q0 + jnp.arange(Q_BLOCK))[:, None] >= kpos[None, :]
        probs = jax.nn.softmax(jnp.where(mask, logits, -jnp.inf), axis=-1)
        return jnp.einsum('bhqk,bkhd->bqhd', probs, v)

    out = lax.map(block, jnp.arange(s // Q_BLOCK))
    return jnp.moveaxis(out, 0, 1).reshape(b, s, h, d)


def fox_sample(q, k, v, logf, k_past, v_past, logf_past):
    p_len, l = k_past.shape[1], q.shape[1]
    kk = jnp.concatenate([k_past.astype(F32), k.astype(F32)], axis=1)
    vv = jnp.concatenate([v_past.astype(F32), v.astype(F32)], axis=1)
    lf = jnp.concatenate([logf_past.astype(F32), logf], axis=1)
    cum = jnp.cumsum(lf, axis=1).transpose(0, 2, 1)
    logits = (jnp.einsum('bqhd,bkhd->bhqk', q.astype(F32), kk) * DH_FOX ** -0.5
              + cum[:, :, p_len:, None] - cum[:, :, None, :])
    mask = (p_len + jnp.arange(l))[:, None] >= jnp.arange(p_len + l)[None, :]
    probs = jax.nn.softmax(jnp.where(mask, logits, -jnp.inf), axis=-1)
    return jnp.einsum('bhqk,bkhd->bqhd', probs, vv)


def mlstm_chunk(carry, xs):
    c_st, n_st, m_st = carry
    q, k, v, ig, lf = xs
    l = q.shape[1]
    bcum = jnp.cumsum(lf, axis=1).transpose(0, 2, 1)
    it = ig.transpose(0, 2, 1)
    causal = jnp.tril(jnp.ones((l, l), bool))
    dlog = jnp.where(causal, bcum[..., :, None] - bcum[..., None, :] + it[..., None, :], -jnp.inf)
    inter = bcum + m_st[..., None]
    m_t = jnp.maximum(inter, dlog.max(axis=-1))
    w_intra = jnp.exp(dlog - m_t[..., None])
    w_inter = jnp.exp(inter - m_t)
    sc = jnp.einsum('bthd,bshd->bhts', q, k) * w_intra
    num = (jnp.einsum('bhts,bshd->bhtd', sc, v)
           + w_inter[..., None] * jnp.einsum('bthk,bhkv->bhtv', q, c_st))
    den = sc.sum(axis=-1) + w_inter * jnp.einsum('bthk,bhk->bht', q, n_st)
    h = num / jnp.maximum(jnp.abs(den), jnp.exp(-m_t))[..., None]
    m_new = m_t[..., -1]
    w_last = jnp.exp(bcum[..., -1:] - bcum + it - m_new[..., None])
    decay = jnp.exp(bcum[..., -1] + m_st - m_new)
    c_new = decay[..., None, None] * c_st + jnp.einsum('bhs,bshk,bshv->bhkv', w_last, k, v)
    n_new = decay[..., None] * n_st + jnp.einsum('bhs,bshk->bhk', w_last, k)
    return (c_new, n_new, m_new), h.transpose(0, 2, 1, 3)


def mlstm_prompt(q, k, v, ig, lf):
    b, s, h, d = q.shape
    nc = s // ML_CHUNK

    def chunks(a):
        return jnp.moveaxis(a.reshape((b, nc, ML_CHUNK) + a.shape[2:]), 1, 0)

    init = (jnp.zeros((b, h, d, d), F32), jnp.zeros((b, h, d), F32), jnp.zeros((b, h), F32))
    state, hs = lax.scan(mlstm_chunk, init, (chunks(q), chunks(k), chunks(v), chunks(ig), chunks(lf)))
    return jnp.moveaxis(hs, 0, 1).reshape(b, s, h, d), state


def mlstm_sample(q, k, v, ig, lf, state):
    st = tuple(a.astype(F32) for a in state)
    new_state, h = mlstm_chunk(st, (q, k, v, ig, lf))
    return h, new_state


def memory_kv(mem, g_norm, w_kv, g_k):
    b, nm, _ = mem.shape
    kv = rmsnorm(mem, g_norm) @ w_kv
    k, v = jnp.split(kv, 2, axis=-1)
    k = rmsnorm(k.reshape(b, nm, H_MEM, DH_MEM), g_k)
    return k, v.reshape(b, nm, H_MEM, DH_MEM).astype(F32)


def attend_memory(q, mk, mv):
    logits = jnp.einsum('bqhd,bmhd->bhqm', q.astype(F32), mk.astype(F32)) * DH_MEM ** -0.5
    probs = jax.nn.softmax(logits, axis=-1)
    return jnp.einsum('bhqm,bmhd->bqhd', probs, mv.astype(F32))


def causal_dwconv(u, prev, w, b):
    l = u.shape[1]
    ext = jnp.concatenate([prev.astype(u.dtype), u], axis=1)
    out = b
    for j in range(CONV_W):
        out = out + w[j] * ext[:, j:j + l]
    return out, ext[:, ext.shape[1] - (CONV_W - 1):]


def trunk_layer(x, p, fox_fn, mlstm_fn, mem_k, mem_v, conv_prev):
    bx, l, _ = x.shape
    offs, _ = _in_offsets()
    z = rmsnorm(x, p['norm1_g']) @ p['w_in'] + p['b_in']
    zs = {n: z[..., s0:s0 + sz] for n, (s0, sz) in offs.items()}
    fq = rmsnorm(zs['fox_q'].reshape(bx, l, H_FOX, DH_FOX), p['fox_qnorm_g'])
    fk = rmsnorm(zs['fox_k'].reshape(bx, l, H_FOX, DH_FOX), p['fox_knorm_g'])
    fv = zs['fox_v'].reshape(bx, l, H_FOX, DH_FOX).astype(F32)
    flogf = jax.nn.log_sigmoid(zs['fox_f'].astype(F32))
    fox_out = fox_fn(fq, fk, fv, flogf).reshape(bx, l, FOX_W)
    mq = zs['ml_q'].reshape(bx, l, H_ML, DH_ML).astype(F32)
    mk = zs['ml_k'].reshape(bx, l, H_ML, DH_ML).astype(F32) * DH_ML ** -0.5
    mv = zs['ml_v'].reshape(bx, l, H_ML, DH_ML).astype(F32)
    mi = zs['ml_i'].astype(F32)
    mlf = jax.nn.log_sigmoid(zs['ml_f'].astype(F32))
    ml_h, ml_state = mlstm_fn(mq, mk, mv, mi, mlf)
    ml_o = jax.nn.sigmoid(zs['ml_o'].astype(F32)).reshape(bx, l, H_ML, DH_ML)
    ml_out = (rmsnorm(ml_h, p['ml_hnorm_g']) * ml_o).reshape(bx, l, ML_W)
    xq = rmsnorm(zs['mem_q'].reshape(bx, l, H_MEM, DH_MEM), p['mem_qnorm_g'])
    mem_out = attend_memory(xq, mem_k, mem_v).reshape(bx, l, MEM_W)
    merged = (jax.nn.sigmoid(zs['g_fox']) * (fox_out @ p['w_br_fox'])
              + jax.nn.sigmoid(zs['g_ml']) * (ml_out @ p['w_br_ml'])
              + jax.nn.sigmoid(zs['g_mem']) * (mem_out @ p['w_br_mem']))
    x = x + (merged @ p['w_out']).astype(x.dtype)
    u = rmsnorm(x, p['norm2_g']) @ p['w_up']
    uc, conv_new = causal_dwconv(u, conv_prev, p['conv_w'], p['conv_b'])
    gate, val = jnp.split(uc, 2, axis=-1)
    x = x + ((jax.nn.silu(gate) * val) @ p['w_down']).astype(x.dtype)
    return x, (fk, fv, flogf), ml_state, conv_new


def setup_inputs(seed: int = 0) -> dict:
    key = jax.random.key(seed)
    ks = iter(jax.random.split(key, 48))

    def nrm(shape, scale=1.0):
        return scale * jax.random.normal(next(ks), shape, F32)

    def gain(shape):
        return 1.0 + 0.02 * jax.random.normal(next(ks), shape, F32)

    n_pages = PAST_LEN // PAGE_SIZE
    n_phys = (DEC_BATCH * n_pages * 5) // 4
    offs, n_in = _in_offsets()
    x_prompt = nrm((BATCH, SEQ, D_MODEL))
    x_sample = nrm((DEC_BATCH, DEC_SEQ, D_MODEL))
    mem_prompt = nrm((BATCH, N_MEM, D_MODEL))
    cache_fox_k = nrm((DEPTH, n_phys, PAGE_SIZE, H_FOX, DH_FOX))
    cache_fox_v = nrm((DEPTH, n_phys, PAGE_SIZE, H_FOX, DH_FOX))
    cache_fox_logf = jax.nn.log_sigmoid(nrm((DEPTH, n_phys, PAGE_SIZE, H_FOX)) + FORGET_BIAS)
    cache_mem_k = nrm((DEPTH, DEC_BATCH, N_MEM, H_MEM, DH_MEM))
    cache_mem_v = nrm((DEPTH, DEC_BATCH, N_MEM, H_MEM, DH_MEM))
    state_mlstm_C = nrm((DEPTH, DEC_BATCH, H_ML, DH_ML, DH_ML), 0.1)
    state_mlstm_n = nrm((DEPTH, DEC_BATCH, H_ML, DH_ML), 0.1)
    state_mlstm_m = nrm((DEPTH, DEC_BATCH, H_ML))
    state_ffn_conv = nrm((DEPTH, DEC_BATCH, CONV_W - 1, 2 * D_FF))
    page_table = jax.random.permutation(next(ks), n_phys)[:DEC_BATCH * n_pages].reshape(
        DEC_BATCH, n_pages).astype(jnp.int32)
    b_in = nrm((DEPTH, n_in), 0.01)
    for name in ('fox_f', 'ml_f'):
        s0, sz = offs[name]
        b_in = b_in.at[:, s0:s0 + sz].add(FORGET_BIAS)
    return {
        'x_prompt': x_prompt, 'x_sample': x_sample, 'mem_prompt': mem_prompt,
        'cache_fox_k': cache_fox_k, 'cache_fox_v': cache_fox_v, 'cache_fox_logf': cache_fox_logf,
        'cache_mem_k': cache_mem_k, 'cache_mem_v': cache_mem_v,
        'state_mlstm_C': state_mlstm_C, 'state_mlstm_n': state_mlstm_n, 'state_mlstm_m': state_mlstm_m,
        'state_ffn_conv': state_ffn_conv, 'page_table': page_table,
        'norm1_g': gain((DEPTH, D_MODEL)),
        'w_in': nrm((DEPTH, D_MODEL, n_in), D_MODEL ** -0.5),
        'b_in': b_in,
        'fox_qnorm_g': gain((DEPTH, DH_FOX)),
        'fox_knorm_g': gain((DEPTH, DH_FOX)),
        'ml_hnorm_g': gain((DEPTH, H_ML, DH_ML)),
        'mem_norm_g': gain((DEPTH, D_MODEL)),
        'w_mem_kv': nrm((DEPTH, D_MODEL, 2 * MEM_W), D_MODEL ** -0.5),
        'mem_qnorm_g': gain((DEPTH, DH_MEM)),
        'mem_knorm_g': gain((DEPTH, DH_MEM)),
        'w_br_fox': nrm((DEPTH, FOX_W, D_MODEL), FOX_W ** -0.5),
        'w_br_ml': nrm((DEPTH, ML_W, D_MODEL), ML_W ** -0.5),
        'w_br_mem': nrm((DEPTH, MEM_W, D_MODEL), MEM_W ** -0.5),
        'w_out': nrm((DEPTH, D_MODEL, D_MODEL), D_MODEL ** -0.5),
        'norm2_g': gain((DEPTH, D_MODEL)),
        'w_up': nrm((DEPTH, D_MODEL, 2 * D_FF), D_MODEL ** -0.5),
        'conv_w': nrm((DEPTH, CONV_W, 2 * D_FF), 0.5),
        'conv_b': nrm((DEPTH, 2 * D_FF), 0.01),
        'w_down': nrm((DEPTH, D_FF, D_MODEL), D_FF ** -0.5),
    }


def reference(x_prompt, x_sample, mem_prompt, cache_fox_k, cache_fox_v, cache_fox_logf,
              cache_mem_k, cache_mem_v, state_mlstm_C, state_mlstm_n, state_mlstm_m, state_ffn_conv,
              page_table, norm1_g, w_in, b_in, fox_qnorm_g, fox_knorm_g, ml_hnorm_g, mem_norm_g,
              w_mem_kv, mem_qnorm_g, mem_knorm_g, w_br_fox, w_br_ml, w_br_mem, w_out, norm2_g,
              w_up, conv_w, conv_b, w_down):
    bp = x_prompt.shape[0]
    db, n_pages = page_table.shape
    past = n_pages * PAGE_SIZE
    yp, ys = x_prompt, x_sample
    fkp, fvp, flp, fks, fvs, fls = [], [], [], [], [], []
    mcp, mnp, mmp, mcs, mns, mms = [], [], [], [], [], []
    mkp, mvp, cvp, cvs = [], [], [], []
    for l in range(DEPTH):
        p = dict(norm1_g=norm1_g[l], w_in=w_in[l], b_in=b_in[l], fox_qnorm_g=fox_qnorm_g[l],
                 fox_knorm_g=fox_knorm_g[l], ml_hnorm_g=ml_hnorm_g[l], mem_qnorm_g=mem_qnorm_g[l],
                 w_br_fox=w_br_fox[l], w_br_ml=w_br_ml[l], w_br_mem=w_br_mem[l], w_out=w_out[l],
                 norm2_g=norm2_g[l], w_up=w_up[l], conv_w=conv_w[l], conv_b=conv_b[l], w_down=w_down[l])
        mk, mv = memory_kv(mem_prompt, mem_norm_g[l], w_mem_kv[l], mem_knorm_g[l])
        conv0 = jnp.zeros((bp, CONV_W - 1, 2 * D_FF), yp.dtype)
        yp, (fk, fv, fl), (mc, mn, mm), cv = trunk_layer(yp, p, fox_prompt, mlstm_prompt, mk, mv, conv0)
        fkp.append(fk); fvp.append(fv); flp.append(fl)
        mcp.append(mc); mnp.append(mn); mmp.append(mm)
        mkp.append(mk); mvp.append(mv); cvp.append(cv)
        k_past = cache_fox_k[l][page_table].reshape(db, past, H_FOX, DH_FOX)
        v_past = cache_fox_v[l][page_table].reshape(db, past, H_FOX, DH_FOX)
        lf_past = cache_fox_logf[l][page_table].reshape(db, past, H_FOX)
        fox_fn = functools.partial(fox_sample, k_past=k_past, v_past=v_past, logf_past=lf_past)
        ml_fn = functools.partial(mlstm_sample, state=(state_mlstm_C[l], state_mlstm_n[l], state_mlstm_m[l]))
        ys, (fk, fv, fl), (mc, mn, mm), cv = trunk_layer(ys, p, fox_fn, ml_fn, cache_mem_k[l], cache_mem_v[l],
                                                         state_ffn_conv[l])
        fks.append(fk); fvs.append(fv); fls.append(fl)
        mcs.append(mc); mns.append(mn); mms.append(mm); cvs.append(cv)
    st = jnp.stack
    return (yp, ys, st(fkp), st(fvp), st(flp), st(fks), st(fvs), st(fls),
            st(mcp), st(mnp), st(mmp), st(mcs), st(mns), st(mms),
            st(mkp), st(mvp), st(cvp), st(cvs))
```

```python
import functools

import jax
import jax.numpy as jnp
from jax import lax
from jax.experimental import pallas as pl
from jax.experimental.pallas import tpu as pltpu

F32 = jnp.float32
BF16 = jnp.bfloat16

H_FOX, DH_FOX = 8, 64
H_ML, DH_ML = 4, 128
H_MEM, DH_MEM = 4, 128
FOX_W = H_FOX * DH_FOX
ML_W = H_ML * DH_ML
MEM_W = H_MEM * DH_MEM
CONV_W = 3
EPS = 1e-6
NEG = -1e30

LANES = 128
V7X_VMEM_LIMIT = 56 * 1024 * 1024
SAMPLE_PAD = 16
ROW_TILE = 256
N_GATES = 16


def _cparams(*sem):
    return pltpu.CompilerParams(dimension_semantics=sem, vmem_limit_bytes=V7X_VMEM_LIMIT)


def _resident(a):
    return pl.BlockSpec(a.shape, lambda *_: (0,) * a.ndim, pipeline_mode=pl.Buffered(1))


def _dot(a, b):
    return jnp.dot(a, b, preferred_element_type=F32)


def _dot_nt(a, b):
    return lax.dot_general(a, b, (((1,), (1,)), ((), ())), preferred_element_type=F32)


def _dot_tn(a, b):
    return lax.dot_general(a, b, (((0,), (0,)), ((), ())), preferred_element_type=F32)


def _log_sigmoid(x):
    return jnp.minimum(x, 0.0) - jnp.log1p(jnp.exp(-jnp.abs(x)))


def _sigmoid(x):
    return 1.0 / (1.0 + jnp.exp(-x))


def _rms(x):
    return x * lax.rsqrt(jnp.mean(x * x, axis=-1, keepdims=True) + EPS)


def _head_norm_lanes(z, dh):
    outs = []
    lane = lax.broadcasted_iota(jnp.int32, (1, LANES), 1)
    for j in range(z.shape[1] // LANES):
        x = z[:, j * LANES:(j + 1) * LANES]
        sq = x * x
        if dh == LANES:
            inv = lax.rsqrt(jnp.sum(sq, axis=-1, keepdims=True) * (1.0 / dh) + EPS)
        else:
            lo = lane < dh
            s_lo = jnp.sum(jnp.where(lo, sq, 0.0), axis=-1, keepdims=True)
            s_hi = jnp.sum(jnp.where(lo, 0.0, sq), axis=-1, keepdims=True)
            inv = jnp.where(lo, lax.rsqrt(s_lo * (1.0 / dh) + EPS), lax.rsqrt(s_hi * (1.0 / dh) + EPS))
        outs.append(x * inv)
    return jnp.concatenate(outs, axis=-1)


def _in_proj_kernel(x_ref, g1_ref, wn_ref, bn_ref, wt_ref, bt_ref, gk_ref, gq_ref, gqc_ref, gmq_ref,
                    *out_refs, transposed_qv):
    if transposed_qv:
        (fk32_ref, fk16_ref, fv32_ref, mq_ref, mk_ref, mv_ref, mo_ref, memq_ref, gates_ref,
         gt_ref, qT_ref, vT_ref) = out_refs
    else:
        (fk32_ref, fk16_ref, fv32_ref, mq_ref, mk_ref, mv_ref, mo_ref, memq_ref, gates_ref,
         gt_ref, fq_ref) = out_refs
    h = (_rms(x_ref[...]) * g1_ref[...]).astype(BF16)

    def slab(i, width=FOX_W):
        return _dot(h, wn_ref[:, i:i + width]) + bn_ref[:, i:i + width]

    fk = _head_norm_lanes(slab(0), DH_FOX) * gk_ref[...]
    fk32_ref[...] = fk
    fk16_ref[...] = fk.astype(BF16)
    fv32_ref[...] = slab(FOX_W)
    mq_ref[...] = slab(2 * FOX_W).astype(BF16)
    mk_ref[...] = (slab(3 * FOX_W) * DH_ML ** -0.5).astype(BF16)
    mv_ref[...] = slab(4 * FOX_W).astype(BF16)
    mo_ref[...] = _sigmoid(slab(5 * FOX_W)).astype(BF16)
    memq_ref[...] = (_head_norm_lanes(slab(6 * FOX_W), DH_MEM) * gmq_ref[...] * DH_MEM ** -0.5).astype(BF16)
    g0 = 7 * FOX_W
    for j in range(3):
        gates_ref[:, j * 1024:(j + 1) * 1024] = _sigmoid(slab(g0 + j * 1024, 1024)).astype(BF16)

    zt = _dot_nt(wt_ref[...], h) + bt_ref[...]
    zg = zt[0:N_GATES]
    row = lax.broadcasted_iota(jnp.int32, (N_GATES, 1), 0)
    is_ml_i = (row >= H_FOX) & (row < H_FOX + H_ML)
    gt_ref[...] = jnp.where(is_ml_i, zg, _log_sigmoid(zg))
    if transposed_qv:
        tm = zt.shape[1]
        zq = zt[N_GATES:N_GATES + FOX_W].reshape(H_FOX, DH_FOX, tm)
        inv = lax.rsqrt(jnp.sum(zq * zq, axis=1, keepdims=True) * (1.0 / DH_FOX) + EPS)
        qn = zq * inv * gqc_ref[...].reshape(1, DH_FOX, 1) * DH_FOX ** -0.5
        qT_ref[0] = qn.reshape(FOX_W, tm).astype(BF16)
        vT_ref[0] = zt[N_GATES + FOX_W:N_GATES + 2 * FOX_W].astype(BF16)
    else:
        fq = _head_norm_lanes(slab(g0 + 3 * 1024), DH_FOX) * gq_ref[...] * DH_FOX ** -0.5
        fq_ref[...] = fq.astype(BF16)


def _in_proj(x, lw, *, transposed_qv, tm):
    t, d = x.shape
    wn, bn, wt, bt = (lw['wn_p'], lw['bn_p'], lw['wt_p'], lw['bt_p']) if transposed_qv else (
        lw['wn_s'], lw['bn_s'], lw['wt_s'], lw['bt_s'])
    nt = t // tm
    rows = lambda i: (i, 0)

    def row_out(width, dtype):
        return jax.ShapeDtypeStruct((t, width), dtype), pl.BlockSpec((tm, width), rows)

    outs = [row_out(FOX_W, F32), row_out(FOX_W, BF16), row_out(FOX_W, F32), row_out(ML_W, BF16),
            row_out(ML_W, BF16), row_out(ML_W, BF16), row_out(ML_W, BF16), row_out(MEM_W, BF16),
            row_out(3 * d, BF16),
            (jax.ShapeDtypeStruct((N_GATES, t), F32), pl.BlockSpec((N_GATES, tm), lambda i: (0, i)))]
    if transposed_qv:
        blk = (jax.ShapeDtypeStruct((nt, FOX_W, tm), BF16), pl.BlockSpec((1, FOX_W, tm), lambda i: (i, 0, 0)))
        outs += [blk, blk]
    else:
        outs += [row_out(FOX_W, BF16)]
    ins = [x, lw['norm1_g'], wn, bn, wt, bt, lw['fox_knorm_g'], lw['fox_qnorm_g'], lw['fox_qnorm_gc'],
           lw['mem_qnorm_g']]
    in_specs = [pl.BlockSpec((tm, d), rows)] + [_resident(a) for a in ins[1:]]
    return pl.pallas_call(
        functools.partial(_in_proj_kernel, transposed_qv=transposed_qv),
        out_shape=[o[0] for o in outs], grid=(nt,), in_specs=in_specs, out_specs=[o[1] for o in outs],
        compiler_params=_cparams("parallel"), name="in_proj")(*ins)


def _scan_kernel(g_ref, o_ref, *, seg_fox, seg_ml, valid):
    x = g_ref[...]
    w = x.shape[1]
    row = lax.broadcasted_iota(jnp.int32, (N_GATES, 1), 0)
    pos = lax.broadcasted_iota(jnp.int32, (1, w), 1)
    is_fox = row < H_FOX
    is_ml_i = (row >= H_FOX) & (row < H_FOX + H_ML)
    if valid is not None:
        pad = (pos % seg_ml) >= valid
        x = jnp.where(pad & is_ml_i, NEG, jnp.where(pad & jnp.logical_not(is_fox), 0.0, x))
    segpos = jnp.where(is_fox, pos % seg_fox, pos % seg_ml)
    sh = 1
    while sh < max(seg_fox, seg_ml):
        take = (segpos >= sh) & jnp.logical_not(is_ml_i)
        x = x + jnp.where(take, pltpu.roll(x, sh, axis=1), 0.0)
        sh *= 2
    o_ref[...] = x


def _scan(gt, *, width, seg_fox, seg_ml, valid):
    t = gt.shape[1]
    return pl.pallas_call(
        functools.partial(_scan_kernel, seg_fox=seg_fox, seg_ml=seg_ml, valid=valid),
        out_shape=jax.ShapeDtypeStruct(gt.shape, F32), grid=(t // width,),
        in_specs=[pl.BlockSpec((N_GATES, width), lambda i: (0, i))],
        out_specs=pl.BlockSpec((N_GATES, width), lambda i: (0, i)),
        compiler_params=_cparams("parallel"), name="gate_scan")(gt)


def _page_scan_kernel(x_ref, o_ref):
    x = x_ref[...]
    pos = lax.broadcasted_iota(jnp.int32, (1, x.shape[1]), 1)
    sh = 1
    while sh < x.shape[1]:
        x = x + jnp.where(pos >= sh, pltpu.roll(x, sh, axis=1), 0.0)
        sh *= 2
    o_ref[...] = x


def _page_scan(lf_t):
    rows, page = lf_t.shape
    tr = min(4096, rows)
    return pl.pallas_call(
        _page_scan_kernel, out_shape=jax.ShapeDtypeStruct(lf_t.shape, F32), grid=(rows // tr,),
        in_specs=[pl.BlockSpec((tr, page), lambda i: (i, 0))],
        out_specs=pl.BlockSpec((tr, page), lambda i: (i, 0)),
        compiler_params=_cparams("parallel"), name="page_scan")(lf_t)


def _fox_prompt_kernel(qT_ref, k_ref, vT_ref, crow_ref, ccol_ref, o_ref, *, tq, tk):
    qi = pl.program_id(2)
    r = tq // tk
    qT2 = qT_ref[0]
    rows = lax.broadcasted_iota(jnp.int32, (2 * DH_FOX, 1), 0)
    qTm = [jnp.where(rows < DH_FOX, qT2, jnp.zeros_like(qT2)), jnp.where(rows >= DH_FOX, qT2, jnp.zeros_like(qT2))]
    cq = crow_ref[0, 0]
    qcol = qi * tq + lax.broadcasted_iota(jnp.int32, (1, tq), 1)

    def block(kb, carry, masked):
        k0 = pl.multiple_of(kb * tk, tk)
        k2 = k_ref[0, pl.ds(k0, tk), :]
        vT2 = vT_ref[0, kb]
        ck2 = ccol_ref[0, 0, pl.ds(k0, tk), :]
        out = []
        for hh in range(2):
            m, l, acc = carry[hh]
            s = _dot(k2, qTm[hh]) + (cq[hh:hh + 1, :] - ck2[:, hh:hh + 1])
            if masked:
                krow = k0 + lax.broadcasted_iota(jnp.int32, (tk, 1), 0)
                s = jnp.where(krow <= qcol, s, -jnp.inf)
            m_new = jnp.maximum(m, jnp.max(s, axis=0, keepdims=True))
            alpha = jnp.exp(m - m_new)
            p = jnp.exp(s - m_new)
            l_new = alpha * l + jnp.sum(p, axis=0, keepdims=True)
            pv = _dot(vT2[hh * DH_FOX:(hh + 1) * DH_FOX, :], p.astype(BF16))
            out.append((m_new, l_new, alpha * acc + pv))
        return tuple(out)

    init = tuple((jnp.full((1, tq), NEG, F32), jnp.zeros((1, tq), F32), jnp.zeros((DH_FOX, tq), F32))
                 for _ in range(2))
    carry = lax.fori_loop(0, qi * r, lambda kb, c: block(kb, c, False), init)
    for j in range(r):
        carry = block(qi * r + j, carry, True)
    outT = jnp.concatenate([carry[0][2] / carry[0][1], carry[1][2] / carry[1][1]], axis=0)
    o_ref[0] = outT.T.astype(BF16)


def _fox_prompt(qT, k16, vT, crow, ccol, *, b, s, tq, tk):
    nq, nk = s // tq, s // tk
    n_hp = H_FOX // 2
    return pl.pallas_call(
        functools.partial(_fox_prompt_kernel, tq=tq, tk=tk),
        out_shape=jax.ShapeDtypeStruct((b, s, FOX_W), BF16), grid=(b, n_hp, nq),
        in_specs=[
            pl.BlockSpec((1, 2 * DH_FOX, tq), lambda bi, hp, qi: (bi * nq + qi, hp, 0)),
            pl.BlockSpec((1, s, 2 * DH_FOX), lambda bi, hp, qi: (bi, 0, hp)),
            pl.BlockSpec((1, nk, 2 * DH_FOX, tk), lambda bi, hp, qi: (bi, 0, hp, 0)),
            pl.BlockSpec((1, 1, 2, tq), lambda bi, hp, qi: (bi * nq + qi, hp, 0, 0)),
            pl.BlockSpec((1, 1, s, 2), lambda bi, hp, qi: (bi, hp, 0, 0)),
        ],
        out_specs=pl.BlockSpec((1, tq, 2 * DH_FOX), lambda bi, hp, qi: (bi, qi, hp)),
        compiler_params=_cparams("parallel", "parallel", "arbitrary"), name="fox_prompt")(qT, k16, vT, crow, ccol)


def _fox_sample_kernel(pt_ref, q_ref, kn_ref, vn_ref, crow_ref, ccol_ref, kc_ref, vc_ref, lc_ref, o_ref,
                       kbuf, vbuf, lbuf, sem, m_scr, l_scr, acc_scr, *, layer, n_pages, page, n_valid):
    b = pl.program_id(0)
    nb = pl.num_programs(0)
    tp = q_ref.shape[1]
    rows = H_FOX * tp

    def rep_heads(a):
        return jnp.broadcast_to(a[:, None, :], (H_FOX, tp, a.shape[-1])).reshape(rows, a.shape[-1])

    def copies(bi, j, slot):
        pid = pt_ref[bi, n_pages - 1 - j]
        return (pltpu.make_async_copy(kc_ref.at[layer, pid], kbuf.at[slot], sem.at[0, slot]),
                pltpu.make_async_copy(vc_ref.at[layer, pid], vbuf.at[slot], sem.at[1, slot]),
                pltpu.make_async_copy(lc_ref.at[layer, pid], lbuf.at[slot], sem.at[2, slot]))

    @pl.when(b == 0)
    def _():
        for c in copies(0, 0, 0):
            c.start()

    row_id = lax.broadcasted_iota(jnp.int32, (rows, 1), 0)
    own = (row_id // tp) == (lax.broadcasted_iota(jnp.int32, (1, FOX_W), 1) // DH_FOX)
    q = q_ref[0]
    q_bd = jnp.where(own, jnp.broadcast_to(q[None], (H_FOX, tp, FOX_W)).reshape(rows, FOX_W), jnp.zeros((), BF16))
    cc = ccol_ref[0]
    spos = lax.broadcasted_iota(jnp.int32, (1, tp), 1)
    new_ok = (spos <= (row_id % tp)) & (spos < n_valid)
    s = _dot_nt(q_bd, kn_ref[0]) + (cc - rep_heads(crow_ref[0]))
    s = jnp.where(new_ok, s, -jnp.inf)
    m = jnp.max(s, axis=1, keepdims=True)
    p = jnp.exp(s - m)
    m_scr[...] = m
    l_scr[...] = jnp.sum(p, axis=1, keepdims=True)
    acc_scr[...] = _dot(p.astype(BF16), vn_ref[0].astype(BF16))

    def step(j, suffix):
        slot = (b * n_pages + j) % 2
        for c in copies(b, j, slot):
            c.wait()

        @pl.when(j + 1 < n_pages)
        def _():
            for c in copies(b, j + 1, 1 - slot):
                c.start()

        @pl.when((j + 1 == n_pages) & (b + 1 < nb))
        def _():
            for c in copies(b + 1, 0, 1 - slot):
                c.start()

        lcum = lbuf[slot]
        tot = lcum[:, page - 1:page]
        after = suffix + tot - lcum
        kT = kbuf[slot].reshape(FOX_W, page).astype(BF16)
        vT = vbuf[slot].reshape(FOX_W, page).astype(BF16)
        s = _dot(q_bd, kT) + (cc + rep_heads(after))
        m_old = m_scr[...]
        m_new = jnp.maximum(m_old, jnp.max(s, axis=1, keepdims=True))
        alpha = jnp.exp(m_old - m_new)
        p = jnp.exp(s - m_new)
        m_scr[...] = m_new
        l_scr[...] = alpha * l_scr[...] + jnp.sum(p, axis=1, keepdims=True)
        acc_scr[...] = alpha * acc_scr[...] + _dot_nt(p.astype(BF16), vT)
        return suffix + tot

    lax.fori_loop(0, n_pages, step, jnp.zeros((H_FOX, 1), F32))
    out = jnp.where(own, acc_scr[...] / l_scr[...], 0.0)
    o_ref[0] = sum(out[h * tp:(h + 1) * tp] for h in range(H_FOX)).astype(BF16)


def _fox_sample(layer, n_valid, page_table, q16, kn16, vn32, crow, ccol, kc, vc, lc):
    db, n_pages = page_table.shape
    tp = q16.shape[1]
    page = lc.shape[3]
    rows = H_FOX * tp
    per_seq = lambda bi, pt: (bi, 0, 0)
    grid_spec = pltpu.PrefetchScalarGridSpec(
        num_scalar_prefetch=1, grid=(db,),
        in_specs=[pl.BlockSpec((1, tp, FOX_W), per_seq), pl.BlockSpec((1, tp, FOX_W), per_seq),
                  pl.BlockSpec((1, tp, FOX_W), per_seq), pl.BlockSpec((1, H_FOX, tp), per_seq),
                  pl.BlockSpec((1, rows, 1), per_seq),
                  pl.BlockSpec(memory_space=pl.ANY), pl.BlockSpec(memory_space=pl.ANY),
                  pl.BlockSpec(memory_space=pl.ANY)],
        out_specs=pl.BlockSpec((1, tp, FOX_W), per_seq),
        scratch_shapes=[pltpu.VMEM((2, H_FOX, DH_FOX, page), F32), pltpu.VMEM((2, H_FOX, DH_FOX, page), F32),
                        pltpu.VMEM((2, H_FOX, page), F32), pltpu.SemaphoreType.DMA((3, 2)),
                        pltpu.VMEM((rows, 1), F32), pltpu.VMEM((rows, 1), F32), pltpu.VMEM((rows, FOX_W), F32)])
    return pl.pallas_call(
        functools.partial(_fox_sample_kernel, layer=layer, n_pages=n_pages, page=page, n_valid=n_valid),
        out_shape=jax.ShapeDtypeStruct((db, tp, FOX_W), BF16), grid_spec=grid_spec,
        compiler_params=_cparams("arbitrary"), name="fox_sample")(page_table, q16, kn16, vn32, crow, ccol, kc, vc, lc)


def _mlstm_kernel(q_ref, k_ref, v_ref, o_ref, icol_ref, bcol_ref, irow_ref, brow_ref, c0_ref, m0_ref, g_ref,
                  out_ref, cn_ref, mn_ref, c_scr, m_scr, *, chunk, n_chunks):
    st = pl.program_id(1)

    @pl.when(st == 0)
    def _():
        c_scr[...] = c0_ref[0]
        m_scr[...] = m0_ref[0]

    lane = lax.broadcasted_iota(jnp.int32, (chunk, DH_ML), 1)
    ones_col = jnp.where(lane == 0, 1.0, 0.0).astype(BF16)
    tpos = lax.broadcasted_iota(jnp.int32, (chunk, 1), 0)
    spos = lax.broadcasted_iota(jnp.int32, (1, chunk), 1)
    causal = tpos >= spos
    for c in range(n_chunks):
        rs = slice(c * chunk, (c + 1) * chunk)
        for h in range(H_ML):
            cs = slice(h * DH_ML, (h + 1) * DH_ML)
            q, k, v = q_ref[0, rs, cs], k_ref[0, rs, cs], v_ref[0, rs, cs]
            ic, bc = icol_ref[0, 0, rs, h:h + 1], bcol_ref[0, 0, rs, h:h + 1]
            ir, br = irow_ref[0, 0, h:h + 1, rs], brow_ref[0, 0, h:h + 1, rs]
            m_st = m_scr[h]
            c_aug = c_scr[h]
            dlog = jnp.where(causal, bc - br + ir, -jnp.inf)
            inter = bc + m_st
            m_t = jnp.maximum(inter, jnp.max(dlog, axis=1, keepdims=True))
            w_intra = jnp.exp(dlog - m_t)
            w_inter = jnp.exp(inter - m_t)
            sc = (_dot_nt(q, k) * w_intra).astype(BF16)
            v_aug = jnp.concatenate([v, ones_col], axis=1)
            numden = _dot(sc, v_aug) + w_inter * _dot(q, c_aug.astype(BF16))
            num, den = numden[:, :DH_ML], numden[:, DH_ML:DH_ML + 1]
            hid = num / jnp.maximum(jnp.abs(den), jnp.exp(-m_t))
            m_new = m_t[chunk - 1:chunk, :]
            b_last = bc[chunk - 1:chunk, :]
            w_last = jnp.exp(b_last - bc + ic - m_new)
            decay = jnp.exp(b_last + m_st - m_new)
            kw = (k.astype(F32) * w_last).astype(BF16)
            c_scr[h] = decay * c_aug + _dot_tn(kw, v_aug)
            m_scr[h] = m_new
            out_ref[0, rs, cs] = (_rms(hid) * g_ref[h:h + 1, :] * o_ref[0, rs, cs].astype(F32)).astype(BF16)
    cn_ref[0] = c_scr[...]
    mn_ref[0] = m_scr[...]


def _mlstm(q, k, v, o, icol, bcol, irow, brow, c0, m0, g, *, chunk, n_chunks):
    b, rows, _ = q.shape
    tm = chunk * n_chunks
    steps = rows // tm
    tok = pl.BlockSpec((1, tm, ML_W), lambda bi, si: (bi, si, 0))
    col = pl.BlockSpec((1, 1, tm, H_ML), lambda bi, si: (bi, si, 0, 0))
    rowspec = pl.BlockSpec((1, 1, H_ML, tm), lambda bi, si: (bi, si, 0, 0))
    cspec = pl.BlockSpec((1, H_ML, DH_ML, 2 * DH_ML), lambda bi, si: (bi, 0, 0, 0))
    mspec = pl.BlockSpec((1, H_ML, 1, 1), lambda bi, si: (bi, 0, 0, 0))
    return pl.pallas_call(
        functools.partial(_mlstm_kernel, chunk=chunk, n_chunks=n_chunks),
        out_shape=[jax.ShapeDtypeStruct((b, rows, ML_W), BF16), jax.ShapeDtypeStruct(c0.shape, F32),
                   jax.ShapeDtypeStruct(m0.shape, F32)],
        grid=(b, steps),
        in_specs=[tok, tok, tok, tok, col, col, rowspec, rowspec, cspec, mspec,
                  _resident(g)],
        out_specs=[tok, cspec, mspec],
        scratch_shapes=[pltpu.VMEM((H_ML, DH_ML, 2 * DH_ML), F32), pltpu.VMEM((H_ML, 1, 1), F32)],
        compiler_params=_cparams("parallel", "arbitrary"), name="mlstm")(q, k, v, o, icol, bcol, irow, brow, c0, m0, g)


def _mem_kv_kernel(mem_ref, g_ref, w_ref, gk_ref, k_ref, v_ref):
    h = (_rms(mem_ref[...]) * g_ref[...]).astype(BF16)
    kv = _dot(h, w_ref[...])
    k_ref[...] = _head_norm_lanes(kv[:, :MEM_W], DH_MEM) * gk_ref[...]
    v_ref[...] = kv[:, MEM_W:]


def _mem_kv(mem, g, w16, gk):
    rows, d = mem.shape
    tm = 256
    return pl.pallas_call(
        _mem_kv_kernel,
        out_shape=[jax.ShapeDtypeStruct((rows, MEM_W), F32)] * 2, grid=(rows // tm,),
        in_specs=[pl.BlockSpec((tm, d), lambda i: (i, 0)), _resident(g), _resident(w16), _resident(gk)],
        out_specs=[pl.BlockSpec((tm, MEM_W), lambda i: (i, 0))] * 2,
        compiler_params=_cparams("parallel"), name="mem_kv")(mem, g, w16, gk)


def _mem_attn_kernel(q_ref, k_ref, v_ref, o_ref):
    q = q_ref[0]
    k = k_ref[0].astype(BF16)
    v = v_ref[0].astype(BF16)
    outs = []
    for h in range(H_MEM):
        cs = slice(h * DH_MEM, (h + 1) * DH_MEM)
        s = _dot_nt(q[:, cs], k[:, cs])
        p = jnp.exp(s - jnp.max(s, axis=-1, keepdims=True))
        outs.append(_dot(p.astype(BF16), v[:, cs]) / jnp.sum(p, axis=-1, keepdims=True))
    o_ref[0] = jnp.concatenate(outs, axis=-1).astype(BF16)


def _mem_attn(q, mk, mv, *, tm):
    b, rows, _ = q.shape
    nm = mk.shape[1]
    tok = pl.BlockSpec((1, tm, MEM_W), lambda bi, si: (bi, si, 0))
    mem = pl.BlockSpec((1, nm, MEM_W), lambda bi, si: (bi, 0, 0))
    return pl.pallas_call(
        _mem_attn_kernel, out_shape=jax.ShapeDtypeStruct(q.shape, BF16), grid=(b, rows // tm),
        in_specs=[tok, mem, mem], out_specs=tok,
        compiler_params=_cparams("parallel", "arbitrary"), name="mem_attn")(q, mk, mv)


def _merge_kernel(x_ref, fox_ref, ml_ref, mem_ref, gates_ref, wf_ref, wm_ref, wx_ref, wo_ref, y_ref):
    d = x_ref.shape[1]
    merged = (gates_ref[:, 0:d].astype(F32) * _dot(fox_ref[...], wf_ref[...])
              + gates_ref[:, d:2 * d].astype(F32) * _dot(ml_ref[...], wm_ref[...])
              + gates_ref[:, 2 * d:3 * d].astype(F32) * _dot(mem_ref[...], wx_ref[...]))
    y_ref[...] = x_ref[...] + _dot(merged.astype(BF16), wo_ref[...])


def _merge(x, fox, ml, mem, gates, lw, *, tm):
    t, d = x.shape
    rows = lambda i: (i, 0)
    ws = [lw['w_br_fox'], lw['w_br_ml'], lw['w_br_mem'], lw['w_out']]
    return pl.pallas_call(
        _merge_kernel, out_shape=jax.ShapeDtypeStruct((t, d), F32), grid=(t // tm,),
        in_specs=[pl.BlockSpec((tm, d), rows), pl.BlockSpec((tm, FOX_W), rows), pl.BlockSpec((tm, ML_W), rows),
                  pl.BlockSpec((tm, MEM_W), rows), pl.BlockSpec((tm, 3 * d), rows)]
                 + [_resident(w) for w in ws],
        out_specs=pl.BlockSpec((tm, d), rows),
        compiler_params=_cparams("parallel"), name="merge")(x, fox, ml, mem, gates, *ws)


def _ffn_kernel(x_ref, g_ref, wu_ref, cw_ref, cb_ref, wd_ref, p0_ref, p1_ref, y_ref, aux_ref, carry,
                *, seg, n_split):
    tm = x_ref.shape[-2]
    dff2 = wu_ref.shape[1]
    dff = dff2 // 2
    cw = dff // n_split
    x = x_ref[0]
    h = (_rms(x) * g_ref[...]).astype(BF16)
    row = lax.broadcasted_iota(jnp.int32, (tm, 1), 0)
    if seg is None:
        @pl.when(pl.program_id(1) == 0)
        def _():
            carry[...] = p0_ref[0]
        first, second = row == 0, row == 1
    else:
        first, second = (row % seg) == 0, (row % seg) == 1

    def conv(cols):
        u = _dot(h, wu_ref[:, cols])
        if seg is None:
            hist0, hist1 = carry[0:1, cols], carry[1:2, cols]
            prev1 = jnp.where(first, hist1, pltpu.roll(u, 1, axis=0))
            prev2 = jnp.where(first, hist0, jnp.where(second, hist1, pltpu.roll(u, 2, axis=0)))
            carry[:, cols] = u[tm - 2:tm, :]
        else:
            prev1 = jnp.where(first, p1_ref[:, cols], pltpu.roll(u, 1, axis=0))
            prev2 = jnp.where(first, p0_ref[:, cols],
                              jnp.where(second, pltpu.roll(p1_ref[:, cols], 1, axis=0), pltpu.roll(u, 2, axis=0)))
            aux_ref[:, cols] = u
        return cb_ref[:, cols] + cw_ref[0:1, cols] * prev2 + cw_ref[1:2, cols] * prev1 + cw_ref[2:3, cols] * u

    acc = x
    for j in range(n_split):
        gate = conv(slice(j * cw, (j + 1) * cw))
        val = conv(slice(dff + j * cw, dff + (j + 1) * cw))
        act = (gate * _sigmoid(gate) * val).astype(BF16)
        acc = acc + _dot(act, wd_ref[j * cw:(j + 1) * cw, :])
    y_ref[0] = acc
    if seg is None:
        aux_ref[0] = carry[...]


def _ffn(x, lw, p0, p1, *, seg, tm):
    b, rows, d = x.shape
    dff2 = lw['w_up'].shape[1]
    tok = pl.BlockSpec((1, tm, d), lambda bi, si: (bi, si, 0))
    if seg is None:
        hist = pl.BlockSpec((1, CONV_W - 1, dff2), lambda bi, si: (bi, 0, 0))
        p_specs = [hist, hist]
        aux_shape, aux_spec = jax.ShapeDtypeStruct((b, CONV_W - 1, dff2), F32), hist
    else:
        full = pl.BlockSpec((tm, dff2), lambda bi, si: (si, 0))
        p_specs = [full, full]
        aux_shape, aux_spec = jax.ShapeDtypeStruct((rows, dff2), F32), full
    ws = [lw['norm2_g'], lw['w_up'], lw['conv_w'], lw['conv_b'], lw['w_down']]
    return pl.pallas_call(
        functools.partial(_ffn_kernel, seg=seg, n_split=2),
        out_shape=[jax.ShapeDtypeStruct(x.shape, F32), aux_shape], grid=(b, rows // tm),
        in_specs=[tok] + [_resident(w) for w in ws] + p_specs,
        out_specs=[tok, aux_spec],
        scratch_shapes=[pltpu.VMEM((CONV_W - 1, dff2), F32)],
        compiler_params=_cparams("parallel", "arbitrary"), name="ffn")(x, *ws, p0, p1)


def _in_offsets(d):
    sizes = (('fox_q', FOX_W), ('fox_k', FOX_W), ('fox_v', FOX_W), ('fox_f', H_FOX),
             ('ml_q', ML_W), ('ml_k', ML_W), ('ml_v', ML_W), ('ml_i', H_ML), ('ml_f', H_ML), ('ml_o', ML_W),
             ('mem_q', MEM_W), ('g_fox', d), ('g_ml', d), ('g_mem', d))
    offs, start = {}, 0
    for name, size in sizes:
        offs[name] = (start, size)
        start += size
    return offs


def _layer_weights(l, p):
    w_in, b_in = p['w_in'][l], p['b_in'][l]
    d = w_in.shape[0]
    offs = _in_offsets(d)

    def cols(names):
        w = jnp.concatenate([w_in[:, offs[n][0]:offs[n][0] + offs[n][1]] for n in names], axis=1)
        bias = jnp.concatenate([b_in[offs[n][0]:offs[n][0] + offs[n][1]] for n in names], axis=0)
        return w, bias

    normal = ['fox_k', 'fox_v', 'ml_q', 'ml_k', 'ml_v', 'ml_o', 'mem_q', 'g_fox', 'g_ml', 'g_mem']
    gates = ['fox_f', 'ml_i', 'ml_f']
    wn_p, bn_p = cols(normal)
    wn_s, bn_s = cols(normal + ['fox_q'])
    wt_p, bt_p = cols(gates + ['fox_q', 'fox_v'])
    wt_s, bt_s = cols(gates)
    row = lambda a: a.reshape(1, -1)
    return dict(
        wn_p=wn_p.astype(BF16), bn_p=row(bn_p), wt_p=wt_p.T.astype(BF16), bt_p=bt_p.reshape(-1, 1),
        wn_s=wn_s.astype(BF16), bn_s=row(bn_s), wt_s=wt_s.T.astype(BF16), bt_s=bt_s.reshape(-1, 1),
        norm1_g=row(p['norm1_g'][l]), norm2_g=row(p['norm2_g'][l]),
        fox_knorm_g=row(jnp.tile(p['fox_knorm_g'][l], H_FOX)), fox_qnorm_g=row(jnp.tile(p['fox_qnorm_g'][l], H_FOX)),
        fox_qnorm_gc=p['fox_qnorm_g'][l].reshape(-1, 1),
        mem_qnorm_g=row(jnp.tile(p['mem_qnorm_g'][l], H_MEM)), mem_knorm_g=row(jnp.tile(p['mem_knorm_g'][l], H_MEM)),
        mem_norm_g=row(p['mem_norm_g'][l]), w_mem_kv=p['w_mem_kv'][l].astype(BF16),
        ml_hnorm_g=p['ml_hnorm_g'][l],
        w_br_fox=p['w_br_fox'][l].astype(BF16), w_br_ml=p['w_br_ml'][l].astype(BF16),
        w_br_mem=p['w_br_mem'][l].astype(BF16), w_out=p['w_out'][l].astype(BF16),
        w_up=p['w_up'][l].astype(BF16), w_down=p['w_down'][l].astype(BF16),
        conv_w=p['conv_w'][l], conv_b=row(p['conv_b'][l]))


def _gate_layouts(ct, b, steps, tm):
    i_row = ct[H_FOX:H_FOX + H_ML].reshape(H_ML, b, steps, tm).transpose(1, 2, 0, 3)
    b_row = ct[H_FOX + H_ML:].reshape(H_ML, b, steps, tm).transpose(1, 2, 0, 3)
    return i_row, b_row, i_row.transpose(0, 1, 3, 2), b_row.transpose(0, 1, 3, 2)


def _state_aug(c, n):
    return jnp.concatenate([c, n[..., None], jnp.zeros(c.shape[:-1] + (DH_ML - 1,), F32)], axis=-1)


def _layer_prompt(x, mem, lw):
    b, s, d = x.shape
    t = b * s
    tm = min(ROW_TILE, s)
    (fk32, fk16, fv32, mq, mk, mv, mo, memq, gates, gt, qT, vT) = _in_proj(
        x.reshape(t, d), lw, transposed_qv=True, tm=tm)
    ct = _scan(gt, width=s, seg_fox=s, seg_ml=min(LANES, s), valid=None)
    n_hp = H_FOX // 2
    fcum = ct[:H_FOX]
    crow = fcum.reshape(n_hp, 2, t // tm, tm).transpose(2, 0, 1, 3)
    ccol = fcum.reshape(n_hp, 2, b, s).transpose(2, 0, 3, 1)
    fox = _fox_prompt(qT, fk16.reshape(b, s, FOX_W), vT.reshape(b, s // tm, FOX_W, tm), crow, ccol,
                      b=b, s=s, tq=tm, tk=tm)
    chunk = min(LANES, s)
    n_chunks = max(1, min(4, s // chunk))
    m_tm = chunk * n_chunks
    i_row, b_row, i_col, b_col = _gate_layouts(ct, b, s // m_tm, m_tm)
    r3 = lambda a: a.reshape(b, s, -1)
    c0 = jnp.zeros((b, H_ML, DH_ML, 2 * DH_ML), F32)
    m0 = jnp.zeros((b, H_ML, 1, 1), F32)
    ml, c_new, m_new = _mlstm(r3(mq), r3(mk), r3(mv), r3(mo), i_col, b_col, i_row, b_row, c0, m0,
                              lw['ml_hnorm_g'], chunk=chunk, n_chunks=n_chunks)
    nm = mem.shape[1]
    mk32, mv32 = _mem_kv(mem.reshape(b * nm, d), lw['mem_norm_g'], lw['w_mem_kv'], lw['mem_knorm_g'])
    mk32, mv32 = mk32.reshape(b, nm, MEM_W), mv32.reshape(b, nm, MEM_W)
    mem_out = _mem_attn(r3(memq), mk32, mv32, tm=tm)
    x1 = _merge(x.reshape(t, d), fox.reshape(t, FOX_W), ml.reshape(t, ML_W), mem_out.reshape(t, MEM_W), gates,
                lw, tm=tm)
    dff2 = lw['w_up'].shape[1]
    hist = jnp.zeros((b, CONV_W - 1, dff2), F32)
    y, conv_new = _ffn(x1.reshape(b, s, d), lw, hist, hist, seg=None, tm=tm)
    state = dict(
        fk=fk32.reshape(b, s, H_FOX, DH_FOX), fv=fv32.reshape(b, s, H_FOX, DH_FOX),
        fl=fcum_to_logf(gt, b, s),
        mc=c_new[..., :DH_ML], mn=c_new[..., DH_ML], mm=m_new.reshape(b, H_ML),
        mk=mk32.reshape(b, nm, H_MEM, DH_MEM), mv=mv32.reshape(b, nm, H_MEM, DH_MEM), cv=conv_new)
    return y, state


def fcum_to_logf(gt, b, s):
    return gt[:H_FOX].reshape(H_FOX, b, s).transpose(1, 2, 0)


def _layer_sample(layer, x, lw, page_table, kc, vc, lc, mem_k, mem_v, c_st, n_st, m_st, conv_st, n_valid):
    db, tp, d = x.shape
    t = db * tp
    (fk32, fk16, fv32, mq, mk, mv, mo, memq, gates, gt, fq) = _in_proj(
        x.reshape(t, d), lw, transposed_qv=False, tm=min(ROW_TILE, t))
    ct = _scan(gt, width=t, seg_fox=tp, seg_ml=tp, valid=n_valid)
    r3 = lambda a: a.reshape(db, tp, -1)
    fcum = ct[:H_FOX].reshape(H_FOX, db, tp)
    crow = fcum.transpose(1, 0, 2)
    fox = _fox_sample(layer, n_valid, page_table, r3(fq), r3(fk16), r3(fv32), crow,
                      crow.reshape(db, H_FOX * tp, 1), kc, vc, lc)
    i_row, b_row, i_col, b_col = _gate_layouts(ct, db, 1, tp)
    ml, c_new, m_new = _mlstm(r3(mq), r3(mk), r3(mv), r3(mo), i_col, b_col, i_row, b_row,
                              _state_aug(c_st, n_st), m_st.reshape(db, H_ML, 1, 1), lw['ml_hnorm_g'],
                              chunk=tp, n_chunks=1)
    nm = mem_k.shape[1]
    mem_out = _mem_attn(r3(memq), mem_k.reshape(db, nm, MEM_W), mem_v.reshape(db, nm, MEM_W), tm=tp)
    tm = min(2 * ROW_TILE, t)
    x1 = _merge(x.reshape(t, d), fox.reshape(t, FOX_W), ml.reshape(t, ML_W), mem_out.reshape(t, MEM_W), gates,
                lw, tm=tm)
    pad_rows = lambda a: jnp.pad(a, ((0, 0), (0, tp - 1), (0, 0))).reshape(t, -1)
    p0, p1 = pad_rows(conv_st[:, 0:1]), pad_rows(conv_st[:, 1:2])
    y, u = _ffn(x1.reshape(1, t, d), lw, p0, p1, seg=tp, tm=min(LANES, t))
    v4 = lambda a: a.reshape((db, tp) + a.shape[1:])[:, :n_valid]
    state = dict(
        fk=v4(fk32).reshape(db, n_valid, H_FOX, DH_FOX), fv=v4(fv32).reshape(db, n_valid, H_FOX, DH_FOX),
        fl=fcum_to_logf(gt, db, tp)[:, :n_valid],
        mc=c_new[..., :DH_ML], mn=c_new[..., DH_ML], mm=m_new.reshape(db, H_ML),
        cv=v4(u)[:, n_valid - (CONV_W - 1):])
    return y.reshape(db, tp, d), state


def kernel(x_prompt, x_sample, mem_prompt, cache_fox_k, cache_fox_v, cache_fox_logf, cache_mem_k, cache_mem_v,
           state_mlstm_C, state_mlstm_n, state_mlstm_m, state_ffn_conv, page_table, norm1_g, w_in, b_in,
           fox_qnorm_g, fox_knorm_g, ml_hnorm_g, mem_norm_g, w_mem_kv, mem_qnorm_g, mem_knorm_g, w_br_fox,
           w_br_ml, w_br_mem, w_out, norm2_g, w_up, conv_w, conv_b, w_down):
    params = dict(norm1_g=norm1_g, w_in=w_in, b_in=b_in, fox_qnorm_g=fox_qnorm_g, fox_knorm_g=fox_knorm_g,
                  ml_hnorm_g=ml_hnorm_g, mem_norm_g=mem_norm_g, w_mem_kv=w_mem_kv, mem_qnorm_g=mem_qnorm_g,
                  mem_knorm_g=mem_knorm_g, w_br_fox=w_br_fox, w_br_ml=w_br_ml, w_br_mem=w_br_mem, w_out=w_out,
                  norm2_g=norm2_g, w_up=w_up, conv_w=conv_w, conv_b=conv_b, w_down=w_down)
    depth, n_phys, page = cache_fox_logf.shape[:3]
    db, n_valid, d = x_sample.shape
    kc = jnp.transpose(cache_fox_k, (0, 1, 3, 4, 2))
    vc = jnp.transpose(cache_fox_v, (0, 1, 3, 4, 2))
    lf_t = jnp.swapaxes(cache_fox_logf, 2, 3).reshape(depth * n_phys * H_FOX, page)
    lc = _page_scan(lf_t).reshape(depth, n_phys, H_FOX, page)
    yp = x_prompt
    ys = jnp.pad(x_sample, ((0, 0), (0, SAMPLE_PAD - n_valid), (0, 0)))
    sp, ss = [], []
    for l in range(depth):
        lw = _layer_weights(l, params)
        yp, st = _layer_prompt(yp, mem_prompt, lw)
        sp.append(st)
        ys, st = _layer_sample(l, ys, lw, page_table, kc, vc, lc, cache_mem_k[l], cache_mem_v[l],
                               state_mlstm_C[l], state_mlstm_n[l], state_mlstm_m[l], state_ffn_conv[l], n_valid)
        ss.append(st)
    stack = lambda sts, key: jnp.stack([s[key] for s in sts])
    return (yp, ys[:, :n_valid],
            stack(sp, 'fk'), stack(sp, 'fv'), stack(sp, 'fl'), stack(ss, 'fk'), stack(ss, 'fv'), stack(ss, 'fl'),
            stack(sp, 'mc'), stack(sp, 'mn'), stack(sp, 'mm'), stack(ss, 'mc'), stack(ss, 'mn'), stack(ss, 'mm'),
            stack(sp, 'mk'), stack(sp, 'mv'), stack(sp, 'cv'), stack(ss, 'cv'))
```

```python
import functools

import jax
import jax.numpy as jnp
from jax import lax
from jax.experimental import pallas as pl
from jax.experimental.pallas import tpu as pltpu

F32 = jnp.float32
BF16 = jnp.bfloat16

H_FOX, DH_FOX = 8, 64
H_ML, DH_ML = 4, 128
H_MEM, DH_MEM = 4, 128
FOX_W = H_FOX * DH_FOX
ML_W = H_ML * DH_ML
MEM_W = H_MEM * DH_MEM
CONV_W = 3
EPS = 1e-6
NEG = -1e30
LOG2E = 1.4426950408889634
FOX_Q_SCALE = DH_FOX ** -0.5 * LOG2E

LANES = 128
V7X_VMEM_LIMIT = 56 * 1024 * 1024
SAMPLE_PAD = 16
ROW_TILE = 256
N_GATES = 16
PAGE_GROUP = 8
RING_GROUPS = 3


def _cparams(*sem):
    return pltpu.CompilerParams(dimension_semantics=sem, vmem_limit_bytes=V7X_VMEM_LIMIT)


def _resident(a):
    return pl.BlockSpec(a.shape, lambda *_: (0,) * a.ndim, pipeline_mode=pl.Buffered(1))


def _dot(a, b):
    return jnp.dot(a, b, preferred_element_type=F32)


def _dot_nt(a, b):
    return lax.dot_general(a, b, (((1,), (1,)), ((), ())), preferred_element_type=F32)


def _dot_tn(a, b):
    return lax.dot_general(a, b, (((0,), (0,)), ((), ())), preferred_element_type=F32)


def _split3(x):
    hi = x.astype(BF16).astype(F32)
    r = x - hi
    mid = r.astype(BF16).astype(F32)
    return [hi, mid, (r - mid).astype(BF16).astype(F32)]


def _log_sigmoid(x):
    return jnp.minimum(x, 0.0) - jnp.log1p(jnp.exp(-jnp.abs(x)))


def _sigmoid(x):
    return 1.0 / (1.0 + jnp.exp(-x))


def _rms(x):
    return x * lax.rsqrt(jnp.mean(x * x, axis=-1, keepdims=True) + EPS)


def _head_norm_lanes(z, dh):
    outs = []
    lane = lax.broadcasted_iota(jnp.int32, (1, LANES), 1)
    for j in range(z.shape[1] // LANES):
        x = z[:, j * LANES:(j + 1) * LANES]
        sq = x * x
        if dh == LANES:
            inv = lax.rsqrt(jnp.sum(sq, axis=-1, keepdims=True) * (1.0 / dh) + EPS)
        else:
            lo = lane < dh
            s_lo = jnp.sum(jnp.where(lo, sq, 0.0), axis=-1, keepdims=True)
            s_hi = jnp.sum(jnp.where(lo, 0.0, sq), axis=-1, keepdims=True)
            inv = jnp.where(lo, lax.rsqrt(s_lo * (1.0 / dh) + EPS), lax.rsqrt(s_hi * (1.0 / dh) + EPS))
        outs.append(x * inv)
    return jnp.concatenate(outs, axis=-1)


def _in_proj_kernel(x_ref, g1_ref, wn_ref, bn_ref, wt_ref, bt_ref, gk_ref, gq_ref, gqc_ref, gmq_ref,
                    *out_refs, transposed_qv):
    if transposed_qv:
        (fk32_ref, fk16_ref, fv32_ref, mq_ref, mk_ref, mv_ref, mo_ref, memq_ref, gates_ref,
         gt_ref, qT_ref, vT_ref) = out_refs
    else:
        (fk32_ref, fk16_ref, fv32_ref, mq_ref, mk_ref, mv_ref, mo_ref, memq_ref, gates_ref,
         gt_ref, fq_ref) = out_refs
    h = (_rms(x_ref[...]) * g1_ref[...]).astype(BF16)

    def slab(i, width=FOX_W):
        return _dot(h, wn_ref[:, i:i + width]) + bn_ref[:, i:i + width]

    fk = _head_norm_lanes(slab(0), DH_FOX) * gk_ref[...]
    fk32_ref[...] = fk
    fk16_ref[...] = fk.astype(BF16)
    fv32_ref[...] = slab(FOX_W)
    mq_ref[...] = slab(2 * FOX_W).astype(BF16)
    mk_ref[...] = (slab(3 * FOX_W) * DH_ML ** -0.5).astype(BF16)
    mv_ref[...] = slab(4 * FOX_W).astype(BF16)
    mo_ref[...] = _sigmoid(slab(5 * FOX_W)).astype(BF16)
    memq_ref[...] = (_head_norm_lanes(slab(6 * FOX_W), DH_MEM) * gmq_ref[...] * DH_MEM ** -0.5).astype(BF16)
    g0 = 7 * FOX_W
    for j in range(3):
        gates_ref[:, j * 1024:(j + 1) * 1024] = _sigmoid(slab(g0 + j * 1024, 1024)).astype(BF16)

    zt = _dot_nt(wt_ref[...], h) + bt_ref[...]
    zg = zt[0:N_GATES]
    row = lax.broadcasted_iota(jnp.int32, (N_GATES, 1), 0)
    is_ml_i = (row >= H_FOX) & (row < H_FOX + H_ML)
    gt_ref[...] = jnp.where(is_ml_i, zg, _log_sigmoid(zg))
    if transposed_qv:
        tm = zt.shape[1]
        zq = zt[N_GATES:N_GATES + FOX_W].reshape(H_FOX, DH_FOX, tm)
        inv = lax.rsqrt(jnp.sum(zq * zq, axis=1, keepdims=True) * (1.0 / DH_FOX) + EPS)
        qn = zq * inv * gqc_ref[...].reshape(1, DH_FOX, 1) * FOX_Q_SCALE
        qT_ref[0] = qn.reshape(FOX_W, tm).astype(BF16)
        vT_ref[0] = zt[N_GATES + FOX_W:N_GATES + 2 * FOX_W].astype(BF16)
    else:
        fq = _head_norm_lanes(slab(g0 + 3 * 1024), DH_FOX) * gq_ref[...] * FOX_Q_SCALE
        fq_ref[...] = fq.astype(BF16)


def _in_proj(x, lw, *, transposed_qv, tm):
    t, d = x.shape
    wn, bn, wt, bt = (lw['wn_p'], lw['bn_p'], lw['wt_p'], lw['bt_p']) if transposed_qv else (
        lw['wn_s'], lw['bn_s'], lw['wt_s'], lw['bt_s'])
    nt = t // tm
    rows = lambda i: (i, 0)

    def row_out(width, dtype):
        return jax.ShapeDtypeStruct((t, width), dtype), pl.BlockSpec((tm, width), rows)

    outs = [row_out(FOX_W, F32), row_out(FOX_W, BF16), row_out(FOX_W, F32), row_out(ML_W, BF16),
            row_out(ML_W, BF16), row_out(ML_W, BF16), row_out(ML_W, BF16), row_out(MEM_W, BF16),
            row_out(3 * d, BF16),
            (jax.ShapeDtypeStruct((N_GATES, t), F32), pl.BlockSpec((N_GATES, tm), lambda i: (0, i)))]
    if transposed_qv:
        blk = (jax.ShapeDtypeStruct((nt, FOX_W, tm), BF16), pl.BlockSpec((1, FOX_W, tm), lambda i: (i, 0, 0)))
        outs += [blk, blk]
    else:
        outs += [row_out(FOX_W, BF16)]
    ins = [x, lw['norm1_g'], wn, bn, wt, bt, lw['fox_knorm_g'], lw['fox_qnorm_g'], lw['fox_qnorm_gc'],
           lw['mem_qnorm_g']]
    in_specs = [pl.BlockSpec((tm, d), rows)] + [_resident(a) for a in ins[1:]]
    return pl.pallas_call(
        functools.partial(_in_proj_kernel, transposed_qv=transposed_qv),
        out_shape=[o[0] for o in outs], grid=(nt,), in_specs=in_specs, out_specs=[o[1] for o in outs],
        compiler_params=_cparams("parallel"), name="in_proj")(*ins)


def _scan_kernel(g_ref, o_ref, *, seg_fox, seg_ml, valid):
    x = g_ref[...]
    w = x.shape[1]
    row = lax.broadcasted_iota(jnp.int32, (N_GATES, 1), 0)
    pos = lax.broadcasted_iota(jnp.int32, (1, w), 1)
    is_fox = row < H_FOX
    is_ml_i = (row >= H_FOX) & (row < H_FOX + H_ML)
    if valid is not None:
        pad = (pos % seg_ml) >= valid
        x = jnp.where(pad & is_ml_i, NEG, jnp.where(pad & jnp.logical_not(is_fox), 0.0, x))
    segpos = jnp.where(is_fox, pos % seg_fox, pos % seg_ml)
    sh = 1
    while sh < max(seg_fox, seg_ml):
        take = (segpos >= sh) & jnp.logical_not(is_ml_i)
        x = x + jnp.where(take, pltpu.roll(x, sh, axis=1), 0.0)
        sh *= 2
    o_ref[...] = x


def _scan(gt, *, width, seg_fox, seg_ml, valid):
    t = gt.shape[1]
    return pl.pallas_call(
        functools.partial(_scan_kernel, seg_fox=seg_fox, seg_ml=seg_ml, valid=valid),
        out_shape=jax.ShapeDtypeStruct(gt.shape, F32), grid=(t // width,),
        in_specs=[pl.BlockSpec((N_GATES, width), lambda i: (0, i))],
        out_specs=pl.BlockSpec((N_GATES, width), lambda i: (0, i)),
        compiler_params=_cparams("parallel"), name="gate_scan")(gt)


def _page_scan_kernel(x_ref, o_ref):
    x = x_ref[...]
    pos = lax.broadcasted_iota(jnp.int32, (1, x.shape[1]), 1)
    sh = 1
    while sh < x.shape[1]:
        x = x + jnp.where(pos >= sh, pltpu.roll(x, sh, axis=1), 0.0)
        sh *= 2
    o_ref[...] = x


def _page_scan(lf_t):
    rows, page = lf_t.shape
    tr = min(4096, rows)
    return pl.pallas_call(
        _page_scan_kernel, out_shape=jax.ShapeDtypeStruct(lf_t.shape, F32), grid=(rows // tr,),
        in_specs=[pl.BlockSpec((tr, page), lambda i: (i, 0))],
        out_specs=pl.BlockSpec((tr, page), lambda i: (i, 0)),
        compiler_params=_cparams("parallel"), name="page_scan")(lf_t)


def _fox_prompt_kernel(qT_ref, k_ref, vT_ref, crow_ref, ccol_ref, o_ref, aug_scr, *, tq, tk):
    assert tq == tk
    qi = pl.program_id(2)
    n_feat = 2 * DH_FOX

    @pl.when(qi == 0)
    def _():
        lane = lax.broadcasted_iota(jnp.int32, (1, n_feat), 1)

        def fill(i, carry):
            r0 = pl.multiple_of(i * tk, tk)
            c = ccol_ref[0, 0, pl.ds(r0, tk), :] * (-LOG2E)
            feats = [jnp.ones((tk, 1), F32)] * 3 + _split3(c[:, 0:1]) + _split3(c[:, 1:2])
            a = jnp.zeros((tk, n_feat), F32)
            for j, f in enumerate(feats):
                a = jnp.where(lane == j, f, a)
            aug_scr[pl.ds(r0, tk), :] = a.astype(BF16)
            return carry

        lax.fori_loop(0, k_ref.shape[1] // tk, fill, 0)

    qT2 = qT_ref[0]
    rows = lax.broadcasted_iota(jnp.int32, (n_feat, 1), 0)
    cq = crow_ref[0, 0] * LOG2E
    rhs = []
    for hh in range(2):
        q_own = jnp.where((rows >= hh * DH_FOX) & (rows < (hh + 1) * DH_FOX), qT2, jnp.zeros_like(qT2))
        hi, mid, lo = _split3(cq[hh:hh + 1, :])
        ones_rows = (rows >= 3 + 3 * hh) & (rows < 6 + 3 * hh)
        feat = jnp.where(rows == 0, hi, jnp.where(rows == 1, mid, jnp.where(rows == 2, lo,
                         jnp.where(ones_rows, 1.0, 0.0))))
        rhs.append(jnp.concatenate([q_own, feat.astype(BF16)], axis=0))

    def scores(kb):
        k0 = pl.multiple_of(kb * tk, tk)
        lhs = jnp.concatenate([k_ref[0, pl.ds(k0, tk), :], aug_scr[pl.ds(k0, tk), :]], axis=1)
        return tuple(_dot(lhs, rhs[hh]) for hh in range(2))

    def softmax(s, m, l):
        m_new = jnp.maximum(m, jnp.max(s, axis=0, keepdims=True))
        alpha = jnp.exp2(m - m_new)
        p = jnp.exp2(s - m_new)
        return p.astype(BF16), alpha, m_new, alpha * l + jnp.sum(p, axis=0, keepdims=True)

    def values(kb, p, alpha, acc):
        vT2 = vT_ref[0, kb]
        return tuple(alpha[hh] * acc[hh] + _dot(vT2[hh * DH_FOX:(hh + 1) * DH_FOX, :], p[hh]) for hh in range(2))

    def step(j, carry):
        s, p_prev, a_prev, m, l, acc = carry
        s_next = scores(j + 1)
        sm = [softmax(s[hh], m[hh], l[hh]) for hh in range(2)]
        acc = values(jnp.maximum(j - 1, 0), p_prev, a_prev, acc)
        return (s_next, tuple(x[0] for x in sm), tuple(x[1] for x in sm), tuple(x[2] for x in sm),
                tuple(x[3] for x in sm), acc)

    two = lambda x: (x, x)
    init = (scores(0), two(jnp.zeros((tk, tq), BF16)), two(jnp.ones((1, tq), F32)), two(jnp.full((1, tq), NEG, F32)),
            two(jnp.zeros((1, tq), F32)), two(jnp.zeros((DH_FOX, tq), F32)))
    s, p_prev, a_prev, m, l, acc = lax.fori_loop(0, qi, step, init)
    acc = values(jnp.maximum(qi - 1, 0), p_prev, a_prev, acc)
    krow = qi * tk + lax.broadcasted_iota(jnp.int32, (tk, 1), 0)
    qcol = qi * tq + lax.broadcasted_iota(jnp.int32, (1, tq), 1)
    sm = [softmax(jnp.where(krow <= qcol, s[hh], -jnp.inf), m[hh], l[hh]) for hh in range(2)]
    acc = values(qi, tuple(x[0] for x in sm), tuple(x[1] for x in sm), acc)
    outT = jnp.concatenate([acc[0] / sm[0][3], acc[1] / sm[1][3]], axis=0)
    o_ref[0] = outT.T.astype(BF16)


def _fox_prompt(qT, k16, vT, crow, ccol, *, b, s, tq, tk):
    nq, nk = s // tq, s // tk
    n_hp = H_FOX // 2
    return pl.pallas_call(
        functools.partial(_fox_prompt_kernel, tq=tq, tk=tk),
        scratch_shapes=[pltpu.VMEM((s, 2 * DH_FOX), BF16)],
        out_shape=jax.ShapeDtypeStruct((b, s, FOX_W), BF16), grid=(b, n_hp, nq),
        in_specs=[
            pl.BlockSpec((1, 2 * DH_FOX, tq), lambda bi, hp, qi: (bi * nq + qi, hp, 0)),
            pl.BlockSpec((1, s, 2 * DH_FOX), lambda bi, hp, qi: (bi, 0, hp)),
            pl.BlockSpec((1, nk, 2 * DH_FOX, tk), lambda bi, hp, qi: (bi, 0, hp, 0)),
            pl.BlockSpec((1, 1, 2, tq), lambda bi, hp, qi: (bi * nq + qi, hp, 0, 0)),
            pl.BlockSpec((1, 1, s, 2), lambda bi, hp, qi: (bi, hp, 0, 0)),
        ],
        out_specs=pl.BlockSpec((1, tq, 2 * DH_FOX), lambda bi, hp, qi: (bi, qi, hp)),
        compiler_params=_cparams("parallel", "parallel", "arbitrary"), name="fox_prompt")(qT, k16, vT, crow, ccol)


def _fox_sample_kernel(pt_ref, q_ref, kn_ref, vn_ref, crow_ref, ccol_ref, kc_ref, vc_ref, lc_ref, o_ref,
                       kbuf, vbuf, lbuf, sem, m_scr, l_scr, acc_scr, *, layer, n_seq, n_pages, page, n_valid, group, ring):
    b = pl.program_id(0)
    nb = pl.num_programs(0)
    tp = q_ref.shape[1]
    rows = H_FOX * tp

    def rep_heads(a):
        return jnp.broadcast_to(a[:, None, :], (H_FOX, tp, a.shape[-1])).reshape(rows, a.shape[-1])

    n_groups = n_pages // group

    def group_copies(bi, gi, rslot):
        out = []
        for g in range(group):
            pid = pt_ref[bi, n_pages - 1 - (gi * group + g)]
            slot = rslot * group + g
            out += [pltpu.make_async_copy(kc_ref.at[layer, pid], kbuf.at[slot], sem.at[0, slot]),
                    pltpu.make_async_copy(vc_ref.at[layer, pid], vbuf.at[slot], sem.at[1, slot]),
                    pltpu.make_async_copy(lc_ref.at[layer, pid], lbuf.at[slot], sem.at[2, slot])]
        return out

    @pl.when(b == 0)
    def _():
        for d in range(min(ring - 1, n_seq * n_groups)):
            for c in group_copies(d // n_groups, d % n_groups, d):
                c.start()

    row_id = lax.broadcasted_iota(jnp.int32, (rows, 1), 0)
    own = (row_id // tp) == (lax.broadcasted_iota(jnp.int32, (1, FOX_W), 1) // DH_FOX)
    q = q_ref[0]
    q_bd = jnp.where(own, jnp.broadcast_to(q[None], (H_FOX, tp, FOX_W)).reshape(rows, FOX_W), jnp.zeros((), BF16))
    cc = ccol_ref[0] * LOG2E
    spos = lax.broadcasted_iota(jnp.int32, (1, tp), 1)
    new_ok = (spos <= (row_id % tp)) & (spos < n_valid)
    s = _dot_nt(q_bd, kn_ref[0]) + (cc - rep_heads(crow_ref[0] * LOG2E))
    s = jnp.where(new_ok, s, -jnp.inf)
    m = jnp.max(s, axis=1, keepdims=True)
    p = jnp.exp2(s - m)
    m_scr[...] = m
    l_scr[...] = jnp.sum(p, axis=1, keepdims=True)
    acc_scr[...] = _dot(p.astype(BF16), vn_ref[0].astype(BF16))

    def step(i, suffix):
        gidx = b * n_groups + i
        rslot = gidx % ring
        for c in group_copies(b, i, rslot):
            c.wait()
        nxt = gidx + (ring - 1)
        nxt_b = nxt // n_groups

        @pl.when(nxt_b < nb)
        def _():
            for c in group_copies(nxt_b, nxt % n_groups, nxt % ring):
                c.start()

        after, k_pages, v_pages = [], [], []
        for g in range(group):
            slot = rslot * group + g
            lcum = lbuf[slot]
            tot = lcum[:, page - 1:page]
            after.append(suffix + tot - lcum)
            suffix = suffix + tot
            k_pages.append(kbuf[slot].reshape(FOX_W, page).astype(BF16))
            v_pages.append(vbuf[slot].reshape(FOX_W, page).astype(BF16))
        bias = cc + rep_heads(jnp.concatenate(after, axis=1) * LOG2E)
        s = _dot(q_bd, jnp.concatenate(k_pages, axis=1)) + bias
        m_old = m_scr[...]
        m_new = jnp.maximum(m_old, jnp.max(s, axis=1, keepdims=True))
        alpha = jnp.exp2(m_old - m_new)
        p = jnp.exp2(s - m_new)
        m_scr[...] = m_new
        l_scr[...] = alpha * l_scr[...] + jnp.sum(p, axis=1, keepdims=True)
        acc_scr[...] = alpha * acc_scr[...] + _dot_nt(p.astype(BF16), jnp.concatenate(v_pages, axis=1))
        return suffix

    lax.fori_loop(0, n_groups, step, jnp.zeros((H_FOX, 1), F32))
    out = jnp.where(own, acc_scr[...] / l_scr[...], 0.0)
    o_ref[0] = sum(out[h * tp:(h + 1) * tp] for h in range(H_FOX)).astype(BF16)


def _fox_sample(layer, n_valid, page_table, q16, kn16, vn32, crow, ccol, kc, vc, lc):
    db, n_pages = page_table.shape
    tp = q16.shape[1]
    page = lc.shape[3]
    rows = H_FOX * tp
    group = max(g for g in range(1, PAGE_GROUP + 1) if n_pages % g == 0)
    slots = group * RING_GROUPS
    per_seq = lambda bi, pt: (bi, 0, 0)
    grid_spec = pltpu.PrefetchScalarGridSpec(
        num_scalar_prefetch=1, grid=(db,),
        in_specs=[pl.BlockSpec((1, tp, FOX_W), per_seq), pl.BlockSpec((1, tp, FOX_W), per_seq),
                  pl.BlockSpec((1, tp, FOX_W), per_seq), pl.BlockSpec((1, H_FOX, tp), per_seq),
                  pl.BlockSpec((1, rows, 1), per_seq),
                  pl.BlockSpec(memory_space=pl.ANY), pl.BlockSpec(memory_space=pl.ANY),
                  pl.BlockSpec(memory_space=pl.ANY)],
        out_specs=pl.BlockSpec((1, tp, FOX_W), per_seq),
        scratch_shapes=[pltpu.VMEM((slots, H_FOX, DH_FOX, page), F32), pltpu.VMEM((slots, H_FOX, DH_FOX, page), F32),
                        pltpu.VMEM((slots, H_FOX, page), F32), pltpu.SemaphoreType.DMA((3, slots)),
                        pltpu.VMEM((rows, 1), F32), pltpu.VMEM((rows, 1), F32), pltpu.VMEM((rows, FOX_W), F32)])
    return pl.pallas_call(
        functools.partial(_fox_sample_kernel, layer=layer, n_seq=db, n_pages=n_pages, page=page, n_valid=n_valid,
                          group=group, ring=RING_GROUPS),
        out_shape=jax.ShapeDtypeStruct((db, tp, FOX_W), BF16), grid_spec=grid_spec,
        compiler_params=_cparams("arbitrary"), name="fox_sample")(page_table, q16, kn16, vn32, crow, ccol, kc, vc, lc)


def _mlstm_kernel(q_ref, k_ref, v_ref, o_ref, icol_ref, bcol_ref, irow_ref, brow_ref, c0_ref, m0_ref, g_ref,
                  out_ref, cn_ref, mn_ref, c_scr, m_scr, *, chunk, n_chunks):
    st = pl.program_id(1)

    @pl.when(st == 0)
    def _():
        c_scr[...] = c0_ref[0]
        m_scr[...] = m0_ref[0]

    lane = lax.broadcasted_iota(jnp.int32, (chunk, DH_ML), 1)
    ones_col = jnp.where(lane == 0, 1.0, 0.0).astype(BF16)
    tpos = lax.broadcasted_iota(jnp.int32, (chunk, 1), 0)
    spos = lax.broadcasted_iota(jnp.int32, (1, chunk), 1)
    causal = tpos >= spos
    for c in range(n_chunks):
        rs = slice(c * chunk, (c + 1) * chunk)
        for h in range(H_ML):
            cs = slice(h * DH_ML, (h + 1) * DH_ML)
            q, k, v = q_ref[0, rs, cs], k_ref[0, rs, cs], v_ref[0, rs, cs]
            ic, bc = icol_ref[0, 0, rs, h:h + 1], bcol_ref[0, 0, rs, h:h + 1]
            ir, br = irow_ref[0, 0, h:h + 1, rs], brow_ref[0, 0, h:h + 1, rs]
            m_st = m_scr[h]
            c_aug = c_scr[h]
            dlog = jnp.where(causal, bc - br + ir, -jnp.inf)
            inter = bc + m_st
            m_t = jnp.maximum(inter, jnp.max(dlog, axis=1, keepdims=True))
            w_intra = jnp.exp(dlog - m_t)
            w_inter = jnp.exp(inter - m_t)
            sc = (_dot_nt(q, k) * w_intra).astype(BF16)
            v_aug = jnp.concatenate([v, ones_col], axis=1)
            numden = _dot(sc, v_aug) + w_inter * _dot(q, c_aug.astype(BF16))
            num, den = numden[:, :DH_ML], numden[:, DH_ML:DH_ML + 1]
            hid = num / jnp.maximum(jnp.abs(den), jnp.exp(-m_t))
            m_new = m_t[chunk - 1:chunk, :]
            b_last = bc[chunk - 1:chunk, :]
            w_last = jnp.exp(b_last - bc + ic - m_new)
            decay = jnp.exp(b_last + m_st - m_new)
            kw = (k.astype(F32) * w_last).astype(BF16)
            c_scr[h] = decay * c_aug + _dot_tn(kw, v_aug)
            m_scr[h] = m_new
            out_ref[0, rs, cs] = (_rms(hid) * g_ref[h:h + 1, :] * o_ref[0, rs, cs].astype(F32)).astype(BF16)
    cn_ref[0] = c_scr[...]
    mn_ref[0] = m_scr[...]


def _mlstm(q, k, v, o, icol, bcol, irow, brow, c0, m0, g, *, chunk, n_chunks):
    b, rows, _ = q.shape
    tm = chunk * n_chunks
    steps = rows // tm
    tok = pl.BlockSpec((1, tm, ML_W), lambda bi, si: (bi, si, 0))
    col = pl.BlockSpec((1, 1, tm, H_ML), lambda bi, si: (bi, si, 0, 0))
    rowspec = pl.BlockSpec((1, 1, H_ML, tm), lambda bi, si: (bi, si, 0, 0))
    cspec = pl.BlockSpec((1, H_ML, DH_ML, 2 * DH_ML), lambda bi, si: (bi, 0, 0, 0))
    mspec = pl.BlockSpec((1, H_ML, 1, 1), lambda bi, si: (bi, 0, 0, 0))
    return pl.pallas_call(
        functools.partial(_mlstm_kernel, chunk=chunk, n_chunks=n_chunks),
        out_shape=[jax.ShapeDtypeStruct((b, rows, ML_W), BF16), jax.ShapeDtypeStruct(c0.shape, F32),
                   jax.ShapeDtypeStruct(m0.shape, F32)],
        grid=(b, steps),
        in_specs=[tok, tok, tok, tok, col, col, rowspec, rowspec, cspec, mspec,
                  _resident(g)],
        out_specs=[tok, cspec, mspec],
        scratch_shapes=[pltpu.VMEM((H_ML, DH_ML, 2 * DH_ML), F32), pltpu.VMEM((H_ML, 1, 1), F32)],
        compiler_params=_cparams("parallel", "arbitrary"), name="mlstm")(q, k, v, o, icol, bcol, irow, brow, c0, m0, g)


def _mem_kv_kernel(mem_ref, g_ref, w_ref, gk_ref, k_ref, v_ref):
    h = (_rms(mem_ref[...]) * g_ref[...]).astype(BF16)
    kv = _dot(h, w_ref[...])
    k_ref[...] = _head_norm_lanes(kv[:, :MEM_W], DH_MEM) * gk_ref[...]
    v_ref[...] = kv[:, MEM_W:]


def _mem_kv(mem, g, w16, gk):
    rows, d = mem.shape
    tm = 256
    return pl.pallas_call(
        _mem_kv_kernel,
        out_shape=[jax.ShapeDtypeStruct((rows, MEM_W), F32)] * 2, grid=(rows // tm,),
        in_specs=[pl.BlockSpec((tm, d), lambda i: (i, 0)), _resident(g), _resident(w16), _resident(gk)],
        out_specs=[pl.BlockSpec((tm, MEM_W), lambda i: (i, 0))] * 2,
        compiler_params=_cparams("parallel"), name="mem_kv")(mem, g, w16, gk)


def _mem_attn_kernel(q_ref, k_ref, v_ref, o_ref):
    q = q_ref[0]
    k = k_ref[0].astype(BF16)
    v = v_ref[0].astype(BF16)
    outs = []
    for h in range(H_MEM):
        cs = slice(h * DH_MEM, (h + 1) * DH_MEM)
        s = _dot_nt(q[:, cs], k[:, cs])
        p = jnp.exp(s - jnp.max(s, axis=-1, keepdims=True))
        outs.append(_dot(p.astype(BF16), v[:, cs]) / jnp.sum(p, axis=-1, keepdims=True))
    o_ref[0] = jnp.concatenate(outs, axis=-1).astype(BF16)


def _mem_attn(q, mk, mv, *, tm):
    b, rows, _ = q.shape
    nm = mk.shape[1]
    tok = pl.BlockSpec((1, tm, MEM_W), lambda bi, si: (bi, si, 0))
    mem = pl.BlockSpec((1, nm, MEM_W), lambda bi, si: (bi, 0, 0))
    return pl.pallas_call(
        _mem_attn_kernel, out_shape=jax.ShapeDtypeStruct(q.shape, BF16), grid=(b, rows // tm),
        in_specs=[tok, mem, mem], out_specs=tok,
        compiler_params=_cparams("parallel", "arbitrary"), name="mem_attn")(q, mk, mv)


def _merge_kernel(x_ref, fox_ref, ml_ref, mem_ref, gates_ref, wf_ref, wm_ref, wx_ref, wo_ref, y_ref):
    d = x_ref.shape[1]
    merged = (gates_ref[:, 0:d].astype(F32) * _dot(fox_ref[...], wf_ref[...])
              + gates_ref[:, d:2 * d].astype(F32) * _dot(ml_ref[...], wm_ref[...])
              + gates_ref[:, 2 * d:3 * d].astype(F32) * _dot(mem_ref[...], wx_ref[...]))
    y_ref[...] = x_ref[...] + _dot(merged.astype(BF16), wo_ref[...])


def _merge(x, fox, ml, mem, gates, lw, *, tm):
    t, d = x.shape
    rows = lambda i: (i, 0)
    ws = [lw['w_br_fox'], lw['w_br_ml'], lw['w_br_mem'], lw['w_out']]
    return pl.pallas_call(
        _merge_kernel, out_shape=jax.ShapeDtypeStruct((t, d), F32), grid=(t // tm,),
        in_specs=[pl.BlockSpec((tm, d), rows), pl.BlockSpec((tm, FOX_W), rows), pl.BlockSpec((tm, ML_W), rows),
                  pl.BlockSpec((tm, MEM_W), rows), pl.BlockSpec((tm, 3 * d), rows)]
                 + [_resident(w) for w in ws],
        out_specs=pl.BlockSpec((tm, d), rows),
        compiler_params=_cparams("parallel"), name="merge")(x, fox, ml, mem, gates, *ws)


def _ffn_kernel(x_ref, g_ref, wu_ref, cw_ref, cb_ref, wd_ref, p0_ref, p1_ref, y_ref, aux_ref, carry,
                *, seg, n_split):
    tm = x_ref.shape[-2]
    dff2 = wu_ref.shape[1]
    dff = dff2 // 2
    cw = dff // n_split
    x = x_ref[0]
    h = (_rms(x) * g_ref[...]).astype(BF16)
    row = lax.broadcasted_iota(jnp.int32, (tm, 1), 0)
    if seg is None:
        @pl.when(pl.program_id(1) == 0)
        def _():
            carry[...] = p0_ref[0]
        first, second = row == 0, row == 1
    else:
        first, second = (row % seg) == 0, (row % seg) == 1

    def conv(cols):
        u = _dot(h, wu_ref[:, cols])
        if seg is None:
            hist0, hist1 = carry[0:1, cols], carry[1:2, cols]
            prev1 = jnp.where(first, hist1, pltpu.roll(u, 1, axis=0))
            prev2 = jnp.where(first, hist0, jnp.where(second, hist1, pltpu.roll(u, 2, axis=0)))
            carry[:, cols] = u[tm - 2:tm, :]
        else:
            prev1 = jnp.where(first, p1_ref[:, cols], pltpu.roll(u, 1, axis=0))
            prev2 = jnp.where(first, p0_ref[:, cols],
                              jnp.where(second, pltpu.roll(p1_ref[:, cols], 1, axis=0), pltpu.roll(u, 2, axis=0)))
            aux_ref[:, cols] = u
        return cb_ref[:, cols] + cw_ref[0:1, cols] * prev2 + cw_ref[1:2, cols] * prev1 + cw_ref[2:3, cols] * u

    acc = x
    for j in range(n_split):
        gate = conv(slice(j * cw, (j + 1) * cw))
        val = conv(slice(dff + j * cw, dff + (j + 1) * cw))
        act = (gate * _sigmoid(gate) * val).astype(BF16)
        acc = acc + _dot(act, wd_ref[j * cw:(j + 1) * cw, :])
    y_ref[0] = acc
    if seg is None:
        aux_ref[0] = carry[...]


def _ffn(x, lw, p0, p1, *, seg, tm):
    b, rows, d = x.shape
    dff2 = lw['w_up'].shape[1]
    tok = pl.BlockSpec((1, tm, d), lambda bi, si: (bi, si, 0))
    if seg is None:
        hist = pl.BlockSpec((1, CONV_W - 1, dff2), lambda bi, si: (bi, 0, 0))
        p_specs = [hist, hist]
        aux_shape, aux_spec = jax.ShapeDtypeStruct((b, CONV_W - 1, dff2), F32), hist
    else:
        full = pl.BlockSpec((tm, dff2), lambda bi, si: (si, 0))
        p_specs = [full, full]
        aux_shape, aux_spec = jax.ShapeDtypeStruct((rows, dff2), F32), full
    ws = [lw['norm2_g'], lw['w_up'], lw['conv_w'], lw['conv_b'], lw['w_down']]
    return pl.pallas_call(
        functools.partial(_ffn_kernel, seg=seg, n_split=2),
        out_shape=[jax.ShapeDtypeStruct(x.shape, F32), aux_shape], grid=(b, rows // tm),
        in_specs=[tok] + [_resident(w) for w in ws] + p_specs,
        out_specs=[tok, aux_spec],
        scratch_shapes=[pltpu.VMEM((CONV_W - 1, dff2), F32)],
        compiler_params=_cparams("parallel", "arbitrary"), name="ffn")(x, *ws, p0, p1)


def _in_offsets(d):
    sizes = (('fox_q', FOX_W), ('fox_k', FOX_W), ('fox_v', FOX_W), ('fox_f', H_FOX),
             ('ml_q', ML_W), ('ml_k', ML_W), ('ml_v', ML_W), ('ml_i', H_ML), ('ml_f', H_ML), ('ml_o', ML_W),
             ('mem_q', MEM_W), ('g_fox', d), ('g_ml', d), ('g_mem', d))
    offs, start = {}, 0
    for name, size in sizes:
        offs[name] = (start, size)
        start += size
    return offs


def _layer_weights(l, p):
    w_in, b_in = p['w_in'][l], p['b_in'][l]
    d = w_in.shape[0]
    offs = _in_offsets(d)

    def cols(names):
        w = jnp.concatenate([w_in[:, offs[n][0]:offs[n][0] + offs[n][1]] for n in names], axis=1)
        bias = jnp.concatenate([b_in[offs[n][0]:offs[n][0] + offs[n][1]] for n in names], axis=0)
        return w, bias

    normal = ['fox_k', 'fox_v', 'ml_q', 'ml_k', 'ml_v', 'ml_o', 'mem_q', 'g_fox', 'g_ml', 'g_mem']
    gates = ['fox_f', 'ml_i', 'ml_f']
    wn_p, bn_p = cols(normal)
    wn_s, bn_s = cols(normal + ['fox_q'])
    wt_p, bt_p = cols(gates + ['fox_q', 'fox_v'])
    wt_s, bt_s = cols(gates)
    row = lambda a: a.reshape(1, -1)
    return dict(
        wn_p=wn_p.astype(BF16), bn_p=row(bn_p), wt_p=wt_p.T.astype(BF16), bt_p=bt_p.reshape(-1, 1),
        wn_s=wn_s.astype(BF16), bn_s=row(bn_s), wt_s=wt_s.T.astype(BF16), bt_s=bt_s.reshape(-1, 1),
        norm1_g=row(p['norm1_g'][l]), norm2_g=row(p['norm2_g'][l]),
        fox_knorm_g=row(jnp.tile(p['fox_knorm_g'][l], H_FOX)), fox_qnorm_g=row(jnp.tile(p['fox_qnorm_g'][l], H_FOX)),
        fox_qnorm_gc=p['fox_qnorm_g'][l].reshape(-1, 1),
        mem_qnorm_g=row(jnp.tile(p['mem_qnorm_g'][l], H_MEM)), mem_knorm_g=row(jnp.tile(p['mem_knorm_g'][l], H_MEM)),
        mem_norm_g=row(p['mem_norm_g'][l]), w_mem_kv=p['w_mem_kv'][l].astype(BF16),
        ml_hnorm_g=p['ml_hnorm_g'][l],
        w_br_fox=p['w_br_fox'][l].astype(BF16), w_br_ml=p['w_br_ml'][l].astype(BF16),
        w_br_mem=p['w_br_mem'][l].astype(BF16), w_out=p['w_out'][l].astype(BF16),
        w_up=p['w_up'][l].astype(BF16), w_down=p['w_down'][l].astype(BF16),
        conv_w=p['conv_w'][l], conv_b=row(p['conv_b'][l]))


def _gate_layouts(ct, b, steps, tm):
    i_row = ct[H_FOX:H_FOX + H_ML].reshape(H_ML, b, steps, tm).transpose(1, 2, 0, 3)
    b_row = ct[H_FOX + H_ML:].reshape(H_ML, b, steps, tm).transpose(1, 2, 0, 3)
    return i_row, b_row, i_row.transpose(0, 1, 3, 2), b_row.transpose(0, 1, 3, 2)


def _state_aug(c, n):
    return jnp.concatenate([c, n[..., None], jnp.zeros(c.shape[:-1] + (DH_ML - 1,), F32)], axis=-1)


def _layer_prompt(x, mem, lw):
    b, s, d = x.shape
    t = b * s
    tm = min(ROW_TILE, s)
    (fk32, fk16, fv32, mq, mk, mv, mo, memq, gates, gt, qT, vT) = _in_proj(
        x.reshape(t, d), lw, transposed_qv=True, tm=tm)
    ct = _scan(gt, width=s, seg_fox=s, seg_ml=min(LANES, s), valid=None)
    n_hp = H_FOX // 2
    fcum = ct[:H_FOX]
    crow = fcum.reshape(n_hp, 2, t // tm, tm).transpose(2, 0, 1, 3)
    ccol = fcum.reshape(n_hp, 2, b, s).transpose(2, 0, 3, 1)
    fox = _fox_prompt(qT, fk16.reshape(b, s, FOX_W), vT.reshape(b, s // tm, FOX_W, tm), crow, ccol,
                      b=b, s=s, tq=tm, tk=tm)
    chunk = min(LANES, s)
    n_chunks = max(1, min(4, s // chunk))
    m_tm = chunk * n_chunks
    i_row, b_row, i_col, b_col = _gate_layouts(ct, b, s // m_tm, m_tm)
    r3 = lambda a: a.reshape(b, s, -1)
    c0 = jnp.zeros((b, H_ML, DH_ML, 2 * DH_ML), F32)
    m0 = jnp.zeros((b, H_ML, 1, 1), F32)
    ml, c_new, m_new = _mlstm(r3(mq), r3(mk), r3(mv), r3(mo), i_col, b_col, i_row, b_row, c0, m0,
                              lw['ml_hnorm_g'], chunk=chunk, n_chunks=n_chunks)
    nm = mem.shape[1]
    mk32, mv32 = _mem_kv(mem.reshape(b * nm, d), lw['mem_norm_g'], lw['w_mem_kv'], lw['mem_knorm_g'])
    mk32, mv32 = mk32.reshape(b, nm, MEM_W), mv32.reshape(b, nm, MEM_W)
    mem_out = _mem_attn(r3(memq), mk32, mv32, tm=tm)
    x1 = _merge(x.reshape(t, d), fox.reshape(t, FOX_W), ml.reshape(t, ML_W), mem_out.reshape(t, MEM_W), gates,
                lw, tm=tm)
    dff2 = lw['w_up'].shape[1]
    hist = jnp.zeros((b, CONV_W - 1, dff2), F32)
    y, conv_new = _ffn(x1.reshape(b, s, d), lw, hist, hist, seg=None, tm=tm)
    state = dict(
        fk=fk32.reshape(b, s, H_FOX, DH_FOX), fv=fv32.reshape(b, s, H_FOX, DH_FOX),
        fl=fcum_to_logf(gt, b, s),
        mc=c_new[..., :DH_ML], mn=c_new[..., DH_ML], mm=m_new.reshape(b, H_ML),
        mk=mk32.reshape(b, nm, H_MEM, DH_MEM), mv=mv32.reshape(b, nm, H_MEM, DH_MEM), cv=conv_new)
    return y, state


def fcum_to_logf(gt, b, s):
    return gt[:H_FOX].reshape(H_FOX, b, s).transpose(1, 2, 0)


def _layer_sample(layer, x, lw, page_table, kc, vc, lc, mem_k, mem_v, c_st, n_st, m_st, conv_st, n_valid):
    db, tp, d = x.shape
    t = db * tp
    (fk32, fk16, fv32, mq, mk, mv, mo, memq, gates, gt, fq) = _in_proj(
        x.reshape(t, d), lw, transposed_qv=False, tm=min(ROW_TILE, t))
    ct = _scan(gt, width=t, seg_fox=tp, seg_ml=tp, valid=n_valid)
    r3 = lambda a: a.reshape(db, tp, -1)
    fcum = ct[:H_FOX].reshape(H_FOX, db, tp)
    crow = fcum.transpose(1, 0, 2)
    fox = _fox_sample(layer, n_valid, page_table, r3(fq), r3(fk16), r3(fv32), crow,
                      crow.reshape(db, H_FOX * tp, 1), kc, vc, lc)
    i_row, b_row, i_col, b_col = _gate_layouts(ct, db, 1, tp)
    ml, c_new, m_new = _mlstm(r3(mq), r3(mk), r3(mv), r3(mo), i_col, b_col, i_row, b_row,
                              _state_aug(c_st, n_st), m_st.reshape(db, H_ML, 1, 1), lw['ml_hnorm_g'],
                              chunk=tp, n_chunks=1)
    nm = mem_k.shape[1]
    mem_out = _mem_attn(r3(memq), mem_k.reshape(db, nm, MEM_W), mem_v.reshape(db, nm, MEM_W), tm=tp)
    tm = min(2 * ROW_TILE, t)
    x1 = _merge(x.reshape(t, d), fox.reshape(t, FOX_W), ml.reshape(t, ML_W), mem_out.reshape(t, MEM_W), gates,
                lw, tm=tm)
    pad_rows = lambda a: jnp.pad(a, ((0, 0), (0, tp - 1), (0, 0))).reshape(t, -1)
    p0, p1 = pad_rows(conv_st[:, 0:1]), pad_rows(conv_st[:, 1:2])
    y, u = _ffn(x1.reshape(1, t, d), lw, p0, p1, seg=tp, tm=min(LANES, t))
    v4 = lambda a: a.reshape((db, tp) + a.shape[1:])[:, :n_valid]
    state = dict(
        fk=v4(fk32).reshape(db, n_valid, H_FOX, DH_FOX), fv=v4(fv32).reshape(db, n_valid, H_FOX, DH_FOX),
        fl=fcum_to_logf(gt, db, tp)[:, :n_valid],
        mc=c_new[..., :DH_ML], mn=c_new[..., DH_ML], mm=m_new.reshape(db, H_ML),
        cv=v4(u)[:, n_valid - (CONV_W - 1):])
    return y.reshape(db, tp, d), state


def kernel(x_prompt, x_sample, mem_prompt, cache_fox_k, cache_fox_v, cache_fox_logf, cache_mem_k, cache_mem_v,
           state_mlstm_C, state_mlstm_n, state_mlstm_m, state_ffn_conv, page_table, norm1_g, w_in, b_in,
           fox_qnorm_g, fox_knorm_g, ml_hnorm_g, mem_norm_g, w_mem_kv, mem_qnorm_g, mem_knorm_g, w_br_fox,
           w_br_ml, w_br_mem, w_out, norm2_g, w_up, conv_w, conv_b, w_down):
    params = dict(norm1_g=norm1_g, w_in=w_in, b_in=b_in, fox_qnorm_g=fox_qnorm_g, fox_knorm_g=fox_knorm_g,
                  ml_hnorm_g=ml_hnorm_g, mem_norm_g=mem_norm_g, w_mem_kv=w_mem_kv, mem_qnorm_g=mem_qnorm_g,
                  mem_knorm_g=mem_knorm_g, w_br_fox=w_br_fox, w_br_ml=w_br_ml, w_br_mem=w_br_mem, w_out=w_out,
                  norm2_g=norm2_g, w_up=w_up, conv_w=conv_w, conv_b=conv_b, w_down=w_down)
    depth, n_phys, page = cache_fox_logf.shape[:3]
    db, n_valid, d = x_sample.shape
    kc = jnp.transpose(cache_fox_k, (0, 1, 3, 4, 2))
    vc = jnp.transpose(cache_fox_v, (0, 1, 3, 4, 2))
    lf_t = jnp.swapaxes(cache_fox_logf, 2, 3).reshape(depth * n_phys * H_FOX, page)
    lc = _page_scan(lf_t).reshape(depth, n_phys, H_FOX, page)
    yp = x_prompt
    ys = jnp.pad(x_sample, ((0, 0), (0, SAMPLE_PAD - n_valid), (0, 0)))
    sp, ss = [], []
    for l in range(depth):
        lw = _layer_weights(l, params)
        yp, st = _layer_prompt(yp, mem_prompt, lw)
        sp.append(st)
        ys, st = _layer_sample(l, ys, lw, page_table, kc, vc, lc, cache_mem_k[l], cache_mem_v[l],
                               state_mlstm_C[l], state_mlstm_n[l], state_mlstm_m[l], state_ffn_conv[l], n_valid)
        ss.append(st)
    stack = lambda sts, key: jnp.stack([s[key] for s in sts])
    return (yp, ys[:, :n_valid],
            stack(sp, 'fk'), stack(sp, 'fv'), stack(sp, 'fl'), stack(ss, 'fk'), stack(ss, 'fv'), stack(ss, 'fl'),
            stack(sp, 'mc'), stack(sp, 'mn'), stack(sp, 'mm'), stack(ss, 'mc'), stack(ss, 'mn'), stack(ss, 'mm'),
            stack(sp, 'mk'), stack(sp, 'mv'), stack(sp, 'cv'), stack(ss, 'cv'))
```

```python
import functools

import jax
import jax.numpy as jnp
from jax import lax
from jax.experimental import pallas as pl
from jax.experimental.pallas import tpu as pltpu

F32 = jnp.float32
BF16 = jnp.bfloat16

H_FOX, DH_FOX = 8, 64
H_ML, DH_ML = 4, 128
H_MEM, DH_MEM = 4, 128
FOX_W = H_FOX * DH_FOX
ML_W = H_ML * DH_ML
MEM_W = H_MEM * DH_MEM
CONV_W = 3
EPS = 1e-6
NEG = -1e30
LOG2E = 1.4426950408889634
FOX_Q_SCALE = DH_FOX ** -0.5 * LOG2E

LANES = 128
V7X_VMEM_LIMIT = 56 * 1024 * 1024
SAMPLE_PAD = 16
ROW_TILE = 512
N_GATES = 16
PAGE_GROUP = 8
RING_GROUPS = 3


def _cparams(*sem):
    return pltpu.CompilerParams(dimension_semantics=sem, vmem_limit_bytes=V7X_VMEM_LIMIT)


def _resident(a):
    return pl.BlockSpec(a.shape, lambda *_: (0,) * a.ndim, pipeline_mode=pl.Buffered(1))


def _dot(a, b):
    return jnp.dot(a, b, preferred_element_type=F32)


def _dot_nt(a, b):
    return lax.dot_general(a, b, (((1,), (1,)), ((), ())), preferred_element_type=F32)


def _dot_tn(a, b):
    return lax.dot_general(a, b, (((0,), (0,)), ((), ())), preferred_element_type=F32)


def _split3(x):
    hi = x.astype(BF16).astype(F32)
    r = x - hi
    mid = r.astype(BF16).astype(F32)
    return [hi, mid, (r - mid).astype(BF16).astype(F32)]


def _log_sigmoid(x):
    return jnp.minimum(x, 0.0) - jnp.log1p(jnp.exp(-jnp.abs(x)))


def _sigmoid(x):
    return 1.0 / (1.0 + jnp.exp(-x))


def _rms(x):
    return x * lax.rsqrt(jnp.mean(x * x, axis=-1, keepdims=True) + EPS)


def _head_norm_lanes(z, dh):
    outs = []
    lane = lax.broadcasted_iota(jnp.int32, (1, LANES), 1)
    for j in range(z.shape[1] // LANES):
        x = z[:, j * LANES:(j + 1) * LANES]
        sq = x * x
        if dh == LANES:
            inv = lax.rsqrt(jnp.sum(sq, axis=-1, keepdims=True) * (1.0 / dh) + EPS)
        else:
            lo = lane < dh
            s_lo = jnp.sum(jnp.where(lo, sq, 0.0), axis=-1, keepdims=True)
            s_hi = jnp.sum(jnp.where(lo, 0.0, sq), axis=-1, keepdims=True)
            inv = jnp.where(lo, lax.rsqrt(s_lo * (1.0 / dh) + EPS), lax.rsqrt(s_hi * (1.0 / dh) + EPS))
        outs.append(x * inv)
    return jnp.concatenate(outs, axis=-1)


def _in_proj_kernel(x_ref, g1_ref, wn_ref, bn_ref, wt_ref, bt_ref, gk_ref, gkc_ref, gq_ref, gqc_ref, gmq_ref,
                    *out_refs, transposed_qv):
    if transposed_qv:
        (fk16_ref, mq_ref, mk_ref, mv_ref, mo_ref, memq_ref, gates_ref,
         gt_ref, qT_ref, vT_ref, kT32_ref, vT32_ref) = out_refs
    else:
        (fk16_ref, mq_ref, mk_ref, mv_ref, mo_ref, memq_ref, gates_ref,
         gt_ref, fq_ref, fk32_ref, fv32_ref) = out_refs
    h = (_rms(x_ref[...]) * g1_ref[...]).astype(BF16)

    def slab(i, width=FOX_W):
        return _dot(h, wn_ref[:, i:i + width]) + bn_ref[:, i:i + width]

    fk = _head_norm_lanes(slab(0), DH_FOX) * gk_ref[...]
    fk16_ref[...] = fk.astype(BF16)
    if not transposed_qv:
        fk32_ref[...] = fk
        fv32_ref[...] = slab(FOX_W)
    mq_ref[...] = slab(2 * FOX_W).astype(BF16)
    mk_ref[...] = (slab(3 * FOX_W) * DH_ML ** -0.5).astype(BF16)
    mv_ref[...] = slab(4 * FOX_W).astype(BF16)
    mo_ref[...] = _sigmoid(slab(5 * FOX_W)).astype(BF16)
    memq_ref[...] = (_head_norm_lanes(slab(6 * FOX_W), DH_MEM) * gmq_ref[...] * DH_MEM ** -0.5).astype(BF16)
    g0 = 7 * FOX_W
    for j in range(3):
        gates_ref[:, j * 1024:(j + 1) * 1024] = _sigmoid(slab(g0 + j * 1024, 1024)).astype(BF16)

    zt = _dot_nt(wt_ref[...], h) + bt_ref[...]
    zg = zt[0:N_GATES]
    row = lax.broadcasted_iota(jnp.int32, (N_GATES, 1), 0)
    is_ml_i = (row >= H_FOX) & (row < H_FOX + H_ML)
    gt_ref[...] = jnp.where(is_ml_i, zg, _log_sigmoid(zg))
    if transposed_qv:
        tm = zt.shape[1]

        def head_norm_rows(z, g_col):
            z = z.reshape(H_FOX, DH_FOX, tm)
            inv = lax.rsqrt(jnp.sum(z * z, axis=1, keepdims=True) * (1.0 / DH_FOX) + EPS)
            return (z * inv * g_col.reshape(1, DH_FOX, 1)).reshape(FOX_W, tm)

        qT_ref[0] = (head_norm_rows(zt[N_GATES:N_GATES + FOX_W], gqc_ref[...]) * FOX_Q_SCALE).astype(BF16)
        kT32_ref[0] = head_norm_rows(zt[N_GATES + FOX_W:N_GATES + 2 * FOX_W], gkc_ref[...])
        vT = zt[N_GATES + 2 * FOX_W:N_GATES + 3 * FOX_W]
        vT32_ref[0] = vT
        vT_ref[0] = vT.astype(BF16)
    else:
        fq = _head_norm_lanes(slab(g0 + 3 * 1024), DH_FOX) * gq_ref[...] * FOX_Q_SCALE
        fq_ref[...] = fq.astype(BF16)


def _in_proj(x, lw, *, transposed_qv, tm, seq=None):
    t, d = x.shape
    wn, bn, wt, bt = (lw['wn_p'], lw['bn_p'], lw['wt_p'], lw['bt_p']) if transposed_qv else (
        lw['wn_s'], lw['bn_s'], lw['wt_s'], lw['bt_s'])
    nt = t // tm
    rows = lambda i: (i, 0)

    def row_out(width, dtype):
        return jax.ShapeDtypeStruct((t, width), dtype), pl.BlockSpec((tm, width), rows)

    outs = [row_out(FOX_W, BF16), row_out(ML_W, BF16),
            row_out(ML_W, BF16), row_out(ML_W, BF16), row_out(ML_W, BF16), row_out(MEM_W, BF16),
            row_out(3 * d, BF16),
            (jax.ShapeDtypeStruct((N_GATES, t), F32), pl.BlockSpec((N_GATES, tm), lambda i: (0, i)))]
    if transposed_qv:
        per_seq = seq // tm
        blk = (jax.ShapeDtypeStruct((nt, FOX_W, tm), BF16), pl.BlockSpec((1, FOX_W, tm), lambda i: (i, 0, 0)))
        seq_minor = (jax.ShapeDtypeStruct((t // seq, FOX_W, seq), F32),
                     pl.BlockSpec((1, FOX_W, tm), lambda i: (i // per_seq, 0, i % per_seq)))
        outs += [blk, blk, seq_minor, seq_minor]
    else:
        outs += [row_out(FOX_W, BF16), row_out(FOX_W, F32), row_out(FOX_W, F32)]
    ins = [x, lw['norm1_g'], wn, bn, wt, bt, lw['fox_knorm_g'], lw['fox_knorm_gc'], lw['fox_qnorm_g'],
           lw['fox_qnorm_gc'], lw['mem_qnorm_g']]
    in_specs = [pl.BlockSpec((tm, d), rows)] + [_resident(a) for a in ins[1:]]
    return pl.pallas_call(
        functools.partial(_in_proj_kernel, transposed_qv=transposed_qv),
        out_shape=[o[0] for o in outs], grid=(nt,), in_specs=in_specs, out_specs=[o[1] for o in outs],
        compiler_params=_cparams("parallel"), name="in_proj")(*ins)


def _scan_kernel(g_ref, o_ref, *, seg_fox, seg_ml, valid):
    x = g_ref[...]
    w = x.shape[1]
    row = lax.broadcasted_iota(jnp.int32, (N_GATES, 1), 0)
    pos = lax.broadcasted_iota(jnp.int32, (1, w), 1)
    is_fox = row < H_FOX
    is_ml_i = (row >= H_FOX) & (row < H_FOX + H_ML)
    if valid is not None:
        pad = (pos % seg_ml) >= valid
        x = jnp.where(pad & is_ml_i, NEG, jnp.where(pad & jnp.logical_not(is_fox), 0.0, x))
    segpos = jnp.where(is_fox, pos % seg_fox, pos % seg_ml)
    sh = 1
    while sh < max(seg_fox, seg_ml):
        take = (segpos >= sh) & jnp.logical_not(is_ml_i)
        x = x + jnp.where(take, pltpu.roll(x, sh, axis=1), 0.0)
        sh *= 2
    o_ref[...] = x


def _scan(gt, *, width, seg_fox, seg_ml, valid):
    t = gt.shape[1]
    return pl.pallas_call(
        functools.partial(_scan_kernel, seg_fox=seg_fox, seg_ml=seg_ml, valid=valid),
        out_shape=jax.ShapeDtypeStruct(gt.shape, F32), grid=(t // width,),
        in_specs=[pl.BlockSpec((N_GATES, width), lambda i: (0, i))],
        out_specs=pl.BlockSpec((N_GATES, width), lambda i: (0, i)),
        compiler_params=_cparams("parallel"), name="gate_scan")(gt)


def _page_scan_kernel(x_ref, o_ref):
    x = x_ref[...]
    pos = lax.broadcasted_iota(jnp.int32, (1, x.shape[1]), 1)
    sh = 1
    while sh < x.shape[1]:
        x = x + jnp.where(pos >= sh, pltpu.roll(x, sh, axis=1), 0.0)
        sh *= 2
    o_ref[...] = x


def _page_scan(lf_t):
    rows, page = lf_t.shape
    tr = min(4096, rows)
    return pl.pallas_call(
        _page_scan_kernel, out_shape=jax.ShapeDtypeStruct(lf_t.shape, F32), grid=(rows // tr,),
        in_specs=[pl.BlockSpec((tr, page), lambda i: (i, 0))],
        out_specs=pl.BlockSpec((tr, page), lambda i: (i, 0)),
        compiler_params=_cparams("parallel"), name="page_scan")(lf_t)


def _fox_prompt_kernel(qT_ref, k_ref, vT_ref, crow_ref, ccol_ref, o_ref, aug_scr, rhs_scr, s_a, s_b, mx_a, mx_b,
                       p_scr, m_scr, l_scr, al_scr, acc_scr, *, tq, tk):
    assert tq == tk
    qi = pl.program_id(2)
    n_feat = 2 * DH_FOX

    @pl.when(qi == 0)
    def _():
        lane = lax.broadcasted_iota(jnp.int32, (1, n_feat), 1)

        tr = min(tk, 2 * LANES)

        def fill(i, carry):
            r0 = pl.multiple_of(i * tr, tr)
            c = ccol_ref[0, 0, pl.ds(r0, tr), :] * (-LOG2E)
            feats = [jnp.ones((tr, 1), F32)] * 3 + _split3(c[:, 0:1]) + _split3(c[:, 1:2])
            a = jnp.zeros((tr, n_feat), F32)
            for j, f in enumerate(feats):
                a = jnp.where(lane == j, f, a)
            aug_scr[pl.ds(r0, tr), :] = a.astype(BF16)
            return carry

        lax.fori_loop(0, k_ref.shape[1] // tr, fill, 0)

    qT2 = qT_ref[0]
    rows = lax.broadcasted_iota(jnp.int32, (n_feat, 1), 0)
    cq = crow_ref[0, 0] * LOG2E
    for hh in range(2):
        q_own = jnp.where((rows >= hh * DH_FOX) & (rows < (hh + 1) * DH_FOX), qT2, jnp.zeros_like(qT2))
        hi, mid, lo = _split3(cq[hh:hh + 1, :])
        ones_rows = (rows >= 3 + 3 * hh) & (rows < 6 + 3 * hh)
        feat = jnp.where(rows == 0, hi, jnp.where(rows == 1, mid, jnp.where(rows == 2, lo,
                         jnp.where(ones_rows, 1.0, 0.0))))
        rhs_scr[hh] = jnp.concatenate([q_own, feat.astype(BF16)], axis=0)
    m_scr[...] = jnp.full(m_scr.shape, NEG, F32)
    l_scr[...] = jnp.zeros(l_scr.shape, F32)
    al_scr[...] = jnp.ones(al_scr.shape, F32)
    acc_scr[...] = jnp.zeros(acc_scr.shape, F32)
    p_scr[...] = jnp.zeros(p_scr.shape, BF16)

    def scores(kb, s_ref, mx_ref):
        k0 = pl.multiple_of(kb * tk, tk)
        lhs = jnp.concatenate([k_ref[0, pl.ds(k0, tk), :], aug_scr[pl.ds(k0, tk), :]], axis=1)
        for hh in range(2):
            s = _dot(lhs, rhs_scr[hh])
            s_ref[hh] = s
            mx_ref[hh] = jnp.max(s, axis=0, keepdims=True)

    def values(kb):
        vT2 = vT_ref[0, kb]
        for hh in range(2):
            acc_scr[hh] = al_scr[hh] * acc_scr[hh] + _dot(vT2[hh * DH_FOX:(hh + 1) * DH_FOX, :], p_scr[hh])

    def softmax(hh, s, mx):
        m_old = m_scr[hh]
        m_new = jnp.maximum(m_old, mx)
        alpha = jnp.exp2(m_old - m_new)
        p = jnp.exp2(s - m_new)
        l_scr[hh] = alpha * l_scr[hh] + jnp.sum(p, axis=0, keepdims=True)
        p_scr[hh] = p.astype(BF16)
        m_scr[hh] = m_new
        al_scr[hh] = alpha

    def phase(j, s_cur, mx_cur, s_nxt, mx_nxt):
        values(jnp.maximum(j - 1, 0))
        scores(j + 1, s_nxt, mx_nxt)
        for hh in range(2):
            softmax(hh, s_cur[hh], mx_cur[hh])

    scores(0, s_a, mx_a)

    def trip(t, carry):
        j = 2 * t
        phase(j, s_a, mx_a, s_b, mx_b)

        @pl.when(j + 1 < qi)
        def _():
            phase(j + 1, s_b, mx_b, s_a, mx_a)

        return carry

    lax.fori_loop(0, (qi + 1) // 2, trip, 0)

    def finish(s_ref):
        values(jnp.maximum(qi - 1, 0))
        krow = qi * tk + lax.broadcasted_iota(jnp.int32, (tk, 1), 0)
        qcol = qi * tq + lax.broadcasted_iota(jnp.int32, (1, tq), 1)
        for hh in range(2):
            s = jnp.where(krow <= qcol, s_ref[hh], -jnp.inf)
            softmax(hh, s, jnp.max(s, axis=0, keepdims=True))
        values(qi)
        outT = jnp.concatenate([acc_scr[0] / l_scr[0], acc_scr[1] / l_scr[1]], axis=0)
        o_ref[0] = outT.T.astype(BF16)

    @pl.when(qi % 2 == 0)
    def _():
        finish(s_a)

    @pl.when(qi % 2 == 1)
    def _():
        finish(s_b)


def _fox_prompt(qT, k16, vT, crow, ccol, *, b, s, tq, tk):
    nq, nk = s // tq, s // tk
    n_hp = H_FOX // 2
    pair = lambda shape, dtype: pltpu.VMEM((2,) + shape, dtype)
    return pl.pallas_call(
        functools.partial(_fox_prompt_kernel, tq=tq, tk=tk),
        scratch_shapes=[pltpu.VMEM((s, 2 * DH_FOX), BF16), pair((4 * DH_FOX, tq), BF16),
                        pair((tk, tq), F32), pair((tk, tq), F32), pair((1, tq), F32), pair((1, tq), F32),
                        pair((tk, tq), BF16), pair((1, tq), F32), pair((1, tq), F32), pair((1, tq), F32),
                        pair((DH_FOX, tq), F32)],
        out_shape=jax.ShapeDtypeStruct((b, s, FOX_W), BF16), grid=(b, n_hp, nq),
        in_specs=[
            pl.BlockSpec((1, 2 * DH_FOX, tq), lambda bi, hp, qi: (bi * nq + qi, hp, 0)),
            pl.BlockSpec((1, s, 2 * DH_FOX), lambda bi, hp, qi: (bi, 0, hp)),
            pl.BlockSpec((1, nk, 2 * DH_FOX, tk), lambda bi, hp, qi: (bi, 0, hp, 0)),
            pl.BlockSpec((1, 1, 2, tq), lambda bi, hp, qi: (bi * nq + qi, hp, 0, 0)),
            pl.BlockSpec((1, 1, s, 2), lambda bi, hp, qi: (bi, hp, 0, 0)),
        ],
        out_specs=pl.BlockSpec((1, tq, 2 * DH_FOX), lambda bi, hp, qi: (bi, qi, hp)),
        compiler_params=_cparams("parallel", "parallel", "arbitrary"), name="fox_prompt")(qT, k16, vT, crow, ccol)


def _fox_sample_kernel(pt_ref, q_ref, kn_ref, vn_ref, crow_ref, ccol_ref, kc_ref, vc_ref, lc_ref, o_ref,
                       kbuf, vbuf, lbuf, sem, m_scr, l_scr, acc_scr, *, layer, n_seq, n_pages, page, n_valid, group, ring):
    b = pl.program_id(0)
    nb = pl.num_programs(0)
    tp = q_ref.shape[1]
    rows = H_FOX * tp

    def rep_heads(a):
        return jnp.broadcast_to(a[:, None, :], (H_FOX, tp, a.shape[-1])).reshape(rows, a.shape[-1])

    n_groups = n_pages // group

    def group_copies(bi, gi, rslot):
        out = []
        for g in range(group):
            pid = pt_ref[bi, n_pages - 1 - (gi * group + g)]
            slot = rslot * group + g
            out += [pltpu.make_async_copy(kc_ref.at[layer, pid], kbuf.at[slot], sem.at[0, slot]),
                    pltpu.make_async_copy(vc_ref.at[layer, pid], vbuf.at[slot], sem.at[1, slot]),
                    pltpu.make_async_copy(lc_ref.at[layer, pid], lbuf.at[slot], sem.at[2, slot])]
        return out

    @pl.when(b == 0)
    def _():
        for d in range(min(ring - 1, n_seq * n_groups)):
            for c in group_copies(d // n_groups, d % n_groups, d):
                c.start()

    row_id = lax.broadcasted_iota(jnp.int32, (rows, 1), 0)
    own = (row_id // tp) == (lax.broadcasted_iota(jnp.int32, (1, FOX_W), 1) // DH_FOX)
    q = q_ref[0]
    q_bd = jnp.where(own, jnp.broadcast_to(q[None], (H_FOX, tp, FOX_W)).reshape(rows, FOX_W), jnp.zeros((), BF16))
    cc = ccol_ref[0] * LOG2E
    spos = lax.broadcasted_iota(jnp.int32, (1, tp), 1)
    new_ok = (spos <= (row_id % tp)) & (spos < n_valid)
    s = _dot_nt(q_bd, kn_ref[0]) + (cc - rep_heads(crow_ref[0] * LOG2E))
    s = jnp.where(new_ok, s, -jnp.inf)
    m = jnp.max(s, axis=1, keepdims=True)
    p = jnp.exp2(s - m)
    m_scr[...] = m
    l_scr[...] = jnp.sum(p, axis=1, keepdims=True)
    acc_scr[...] = _dot(p.astype(BF16), vn_ref[0].astype(BF16))

    def step(i, suffix):
        gidx = b * n_groups + i
        rslot = gidx % ring
        for c in group_copies(b, i, rslot):
            c.wait()
        nxt = gidx + (ring - 1)
        nxt_b = nxt // n_groups

        @pl.when(nxt_b < nb)
        def _():
            for c in group_copies(nxt_b, nxt % n_groups, nxt % ring):
                c.start()

        after, k_pages, v_pages = [], [], []
        for g in range(group):
            slot = rslot * group + g
            lcum = lbuf[slot]
            tot = lcum[:, page - 1:page]
            after.append(suffix + tot - lcum)
            suffix = suffix + tot
            k_pages.append(kbuf[slot].reshape(FOX_W, page).astype(BF16))
            v_pages.append(vbuf[slot].reshape(FOX_W, page).astype(BF16))
        bias = cc + rep_heads(jnp.concatenate(after, axis=1) * LOG2E)
        s = _dot(q_bd, jnp.concatenate(k_pages, axis=1)) + bias
        m_old = m_scr[...]
        m_new = jnp.maximum(m_old, jnp.max(s, axis=1, keepdims=True))
        alpha = jnp.exp2(m_old - m_new)
        p = jnp.exp2(s - m_new)
        m_scr[...] = m_new
        l_scr[...] = alpha * l_scr[...] + jnp.sum(p, axis=1, keepdims=True)
        acc_scr[...] = alpha * acc_scr[...] + _dot_nt(p.astype(BF16), jnp.concatenate(v_pages, axis=1))
        return suffix

    lax.fori_loop(0, n_groups, step, jnp.zeros((H_FOX, 1), F32))
    out = jnp.where(own, acc_scr[...] / l_scr[...], 0.0)
    o_ref[0] = sum(out[h * tp:(h + 1) * tp] for h in range(H_FOX)).astype(BF16)


def _fox_sample(layer, n_valid, page_table, q16, kn16, vn32, crow, ccol, kc, vc, lc):
    db, n_pages = page_table.shape
    tp = q16.shape[1]
    page = lc.shape[3]
    rows = H_FOX * tp
    group = max(g for g in range(1, PAGE_GROUP + 1) if n_pages % g == 0)
    slots = group * RING_GROUPS
    per_seq = lambda bi, pt: (bi, 0, 0)
    grid_spec = pltpu.PrefetchScalarGridSpec(
        num_scalar_prefetch=1, grid=(db,),
        in_specs=[pl.BlockSpec((1, tp, FOX_W), per_seq), pl.BlockSpec((1, tp, FOX_W), per_seq),
                  pl.BlockSpec((1, tp, FOX_W), per_seq), pl.BlockSpec((1, H_FOX, tp), per_seq),
                  pl.BlockSpec((1, rows, 1), per_seq),
                  pl.BlockSpec(memory_space=pl.ANY), pl.BlockSpec(memory_space=pl.ANY),
                  pl.BlockSpec(memory_space=pl.ANY)],
        out_specs=pl.BlockSpec((1, tp, FOX_W), per_seq),
        scratch_shapes=[pltpu.VMEM((slots, H_FOX, DH_FOX, page), F32), pltpu.VMEM((slots, H_FOX, DH_FOX, page), F32),
                        pltpu.VMEM((slots, H_FOX, page), F32), pltpu.SemaphoreType.DMA((3, slots)),
                        pltpu.VMEM((rows, 1), F32), pltpu.VMEM((rows, 1), F32), pltpu.VMEM((rows, FOX_W), F32)])
    return pl.pallas_call(
        functools.partial(_fox_sample_kernel, layer=layer, n_seq=db, n_pages=n_pages, page=page, n_valid=n_valid,
                          group=group, ring=RING_GROUPS),
        out_shape=jax.ShapeDtypeStruct((db, tp, FOX_W), BF16), grid_spec=grid_spec,
        compiler_params=_cparams("arbitrary"), name="fox_sample")(page_table, q16, kn16, vn32, crow, ccol, kc, vc, lc)


def _mlstm_kernel(q_ref, k_ref, v_ref, o_ref, icol_ref, bcol_ref, irow_ref, brow_ref, c0_ref, m0_ref, g_ref,
                  out_ref, cn_ref, mn_ref, c_scr, m_scr, *, chunk, n_chunks):
    st = pl.program_id(1)

    @pl.when(st == 0)
    def _():
        c_scr[...] = c0_ref[0]
        m_scr[...] = m0_ref[0]

    lane = lax.broadcasted_iota(jnp.int32, (chunk, DH_ML), 1)
    ones_col = jnp.where(lane == 0, 1.0, 0.0).astype(BF16)
    tpos = lax.broadcasted_iota(jnp.int32, (chunk, 1), 0)
    spos = lax.broadcasted_iota(jnp.int32, (1, chunk), 1)
    causal = tpos >= spos
    for c in range(n_chunks):
        rs = slice(c * chunk, (c + 1) * chunk)
        for h in range(H_ML):
            cs = slice(h * DH_ML, (h + 1) * DH_ML)
            q, k, v = q_ref[0, rs, cs], k_ref[0, rs, cs], v_ref[0, rs, cs]
            ic, bc = icol_ref[0, 0, rs, h:h + 1], bcol_ref[0, 0, rs, h:h + 1]
            ir, br = irow_ref[0, 0, h:h + 1, rs], brow_ref[0, 0, h:h + 1, rs]
            m_st = m_scr[h]
            c_aug = c_scr[h]
            dlog = jnp.where(causal, bc - br + ir, -jnp.inf)
            inter = bc + m_st
            m_t = jnp.maximum(inter, jnp.max(dlog, axis=1, keepdims=True))
            w_intra = jnp.exp(dlog - m_t)
            w_inter = jnp.exp(inter - m_t)
            sc = (_dot_nt(q, k) * w_intra).astype(BF16)
            v_aug = jnp.concatenate([v, ones_col], axis=1)
            numden = _dot(sc, v_aug) + w_inter * _dot(q, c_aug.astype(BF16))
            num, den = numden[:, :DH_ML], numden[:, DH_ML:DH_ML + 1]
            hid = num / jnp.maximum(jnp.abs(den), jnp.exp(-m_t))
            m_new = m_t[chunk - 1:chunk, :]
            b_last = bc[chunk - 1:chunk, :]
            w_last = jnp.exp(b_last - bc + ic - m_new)
            decay = jnp.exp(b_last + m_st - m_new)
            kw = (k.astype(F32) * w_last).astype(BF16)
            c_scr[h] = decay * c_aug + _dot_tn(kw, v_aug)
            m_scr[h] = m_new
            out_ref[0, rs, cs] = (_rms(hid) * g_ref[h:h + 1, :] * o_ref[0, rs, cs].astype(F32)).astype(BF16)
    cn_ref[0] = c_scr[...]
    mn_ref[0] = m_scr[...]


def _mlstm(q, k, v, o, icol, bcol, irow, brow, c0, m0, g, *, chunk, n_chunks):
    b, rows, _ = q.shape
    tm = chunk * n_chunks
    steps = rows // tm
    tok = pl.BlockSpec((1, tm, ML_W), lambda bi, si: (bi, si, 0))
    col = pl.BlockSpec((1, 1, tm, H_ML), lambda bi, si: (bi, si, 0, 0))
    rowspec = pl.BlockSpec((1, 1, H_ML, tm), lambda bi, si: (bi, si, 0, 0))
    cspec = pl.BlockSpec((1, H_ML, DH_ML, 2 * DH_ML), lambda bi, si: (bi, 0, 0, 0))
    mspec = pl.BlockSpec((1, H_ML, 1, 1), lambda bi, si: (bi, 0, 0, 0))
    return pl.pallas_call(
        functools.partial(_mlstm_kernel, chunk=chunk, n_chunks=n_chunks),
        out_shape=[jax.ShapeDtypeStruct((b, rows, ML_W), BF16), jax.ShapeDtypeStruct(c0.shape, F32),
                   jax.ShapeDtypeStruct(m0.shape, F32)],
        grid=(b, steps),
        in_specs=[tok, tok, tok, tok, col, col, rowspec, rowspec, cspec, mspec,
                  _resident(g)],
        out_specs=[tok, cspec, mspec],
        scratch_shapes=[pltpu.VMEM((H_ML, DH_ML, 2 * DH_ML), F32), pltpu.VMEM((H_ML, 1, 1), F32)],
        compiler_params=_cparams("parallel", "arbitrary"), name="mlstm")(q, k, v, o, icol, bcol, irow, brow, c0, m0, g)


def _mem_kv_kernel(mem_ref, g_ref, w_ref, gk_ref, k_ref, v_ref):
    h = (_rms(mem_ref[...]) * g_ref[...]).astype(BF16)
    kv = _dot(h, w_ref[...])
    k_ref[...] = _head_norm_lanes(kv[:, :MEM_W], DH_MEM) * gk_ref[...]
    v_ref[...] = kv[:, MEM_W:]


def _mem_kv(mem, g, w16, gk):
    rows, d = mem.shape
    tm = 256
    return pl.pallas_call(
        _mem_kv_kernel,
        out_shape=[jax.ShapeDtypeStruct((rows, MEM_W), F32)] * 2, grid=(rows // tm,),
        in_specs=[pl.BlockSpec((tm, d), lambda i: (i, 0)), _resident(g), _resident(w16), _resident(gk)],
        out_specs=[pl.BlockSpec((tm, MEM_W), lambda i: (i, 0))] * 2,
        compiler_params=_cparams("parallel"), name="mem_kv")(mem, g, w16, gk)


def _mem_attn_kernel(q_ref, k_ref, v_ref, o_ref):
    q = q_ref[0]
    k = k_ref[0].astype(BF16)
    v = v_ref[0].astype(BF16)
    outs = []
    for h in range(H_MEM):
        cs = slice(h * DH_MEM, (h + 1) * DH_MEM)
        s = _dot_nt(q[:, cs], k[:, cs])
        p = jnp.exp(s - jnp.max(s, axis=-1, keepdims=True))
        outs.append(_dot(p.astype(BF16), v[:, cs]) / jnp.sum(p, axis=-1, keepdims=True))
    o_ref[0] = jnp.concatenate(outs, axis=-1).astype(BF16)


def _mem_attn(q, mk, mv, *, tm):
    b, rows, _ = q.shape
    nm = mk.shape[1]
    tok = pl.BlockSpec((1, tm, MEM_W), lambda bi, si: (bi, si, 0))
    mem = pl.BlockSpec((1, nm, MEM_W), lambda bi, si: (bi, 0, 0))
    return pl.pallas_call(
        _mem_attn_kernel, out_shape=jax.ShapeDtypeStruct(q.shape, BF16), grid=(b, rows // tm),
        in_specs=[tok, mem, mem], out_specs=tok,
        compiler_params=_cparams("parallel", "arbitrary"), name="mem_attn")(q, mk, mv)


def _merge_kernel(x_ref, fox_ref, ml_ref, mem_ref, gates_ref, wf_ref, wm_ref, wx_ref, wo_ref, y_ref):
    d = x_ref.shape[1]
    merged = (gates_ref[:, 0:d].astype(F32) * _dot(fox_ref[...], wf_ref[...])
              + gates_ref[:, d:2 * d].astype(F32) * _dot(ml_ref[...], wm_ref[...])
              + gates_ref[:, 2 * d:3 * d].astype(F32) * _dot(mem_ref[...], wx_ref[...]))
    y_ref[...] = x_ref[...] + _dot(merged.astype(BF16), wo_ref[...])


def _merge(x, fox, ml, mem, gates, lw, *, tm):
    t, d = x.shape
    rows = lambda i: (i, 0)
    ws = [lw['w_br_fox'], lw['w_br_ml'], lw['w_br_mem'], lw['w_out']]
    return pl.pallas_call(
        _merge_kernel, out_shape=jax.ShapeDtypeStruct((t, d), F32), grid=(t // tm,),
        in_specs=[pl.BlockSpec((tm, d), rows), pl.BlockSpec((tm, FOX_W), rows), pl.BlockSpec((tm, ML_W), rows),
                  pl.BlockSpec((tm, MEM_W), rows), pl.BlockSpec((tm, 3 * d), rows)]
                 + [_resident(w) for w in ws],
        out_specs=pl.BlockSpec((tm, d), rows),
        compiler_params=_cparams("parallel"), name="merge")(x, fox, ml, mem, gates, *ws)


def _ffn_kernel(x_ref, g_ref, wu_ref, cw_ref, cb_ref, wd_ref, p0_ref, p1_ref, y_ref, aux_ref, carry,
                *, seg, n_split):
    tm = x_ref.shape[-2]
    dff2 = wu_ref.shape[1]
    dff = dff2 // 2
    cw = dff // n_split
    x = x_ref[0]
    h = (_rms(x) * g_ref[...]).astype(BF16)
    row = lax.broadcasted_iota(jnp.int32, (tm, 1), 0)
    if seg is None:
        @pl.when(pl.program_id(1) == 0)
        def _():
            carry[...] = p0_ref[0]
        first, second = row == 0, row == 1
    else:
        first, second = (row % seg) == 0, (row % seg) == 1

    def conv(cols):
        u = _dot(h, wu_ref[:, cols])
        if seg is None:
            hist0, hist1 = carry[0:1, cols], carry[1:2, cols]
            prev1 = jnp.where(first, hist1, pltpu.roll(u, 1, axis=0))
            prev2 = jnp.where(first, hist0, jnp.where(second, hist1, pltpu.roll(u, 2, axis=0)))
            carry[:, cols] = u[tm - 2:tm, :]
        else:
            prev1 = jnp.where(first, p1_ref[:, cols], pltpu.roll(u, 1, axis=0))
            prev2 = jnp.where(first, p0_ref[:, cols],
                              jnp.where(second, pltpu.roll(p1_ref[:, cols], 1, axis=0), pltpu.roll(u, 2, axis=0)))
            aux_ref[:, cols] = u
        return cb_ref[:, cols] + cw_ref[0:1, cols] * prev2 + cw_ref[1:2, cols] * prev1 + cw_ref[2:3, cols] * u

    acc = x
    for j in range(n_split):
        gate = conv(slice(j * cw, (j + 1) * cw))
        val = conv(slice(dff + j * cw, dff + (j + 1) * cw))
        act = (gate * _sigmoid(gate) * val).astype(BF16)
        acc = acc + _dot(act, wd_ref[j * cw:(j + 1) * cw, :])
    y_ref[0] = acc
    if seg is None:
        aux_ref[0] = carry[...]


def _ffn(x, lw, p0, p1, *, seg, tm):
    b, rows, d = x.shape
    dff2 = lw['w_up'].shape[1]
    tok = pl.BlockSpec((1, tm, d), lambda bi, si: (bi, si, 0))
    if seg is None:
        hist = pl.BlockSpec((1, CONV_W - 1, dff2), lambda bi, si: (bi, 0, 0))
        p_specs = [hist, hist]
        aux_shape, aux_spec = jax.ShapeDtypeStruct((b, CONV_W - 1, dff2), F32), hist
    else:
        full = pl.BlockSpec((tm, dff2), lambda bi, si: (si, 0))
        p_specs = [full, full]
        aux_shape, aux_spec = jax.ShapeDtypeStruct((rows, dff2), F32), full
    ws = [lw['norm2_g'], lw['w_up'], lw['conv_w'], lw['conv_b'], lw['w_down']]
    return pl.pallas_call(
        functools.partial(_ffn_kernel, seg=seg, n_split=2),
        out_shape=[jax.ShapeDtypeStruct(x.shape, F32), aux_shape], grid=(b, rows // tm),
        in_specs=[tok] + [_resident(w) for w in ws] + p_specs,
        out_specs=[tok, aux_spec],
        scratch_shapes=[pltpu.VMEM((CONV_W - 1, dff2), F32)],
        compiler_params=_cparams("parallel", "arbitrary"), name="ffn")(x, *ws, p0, p1)


def _in_offsets(d):
    sizes = (('fox_q', FOX_W), ('fox_k', FOX_W), ('fox_v', FOX_W), ('fox_f', H_FOX),
             ('ml_q', ML_W), ('ml_k', ML_W), ('ml_v', ML_W), ('ml_i', H_ML), ('ml_f', H_ML), ('ml_o', ML_W),
             ('mem_q', MEM_W), ('g_fox', d), ('g_ml', d), ('g_mem', d))
    offs, start = {}, 0
    for name, size in sizes:
        offs[name] = (start, size)
        start += size
    return offs


def _layer_weights(l, p):
    w_in, b_in = p['w_in'][l], p['b_in'][l]
    d = w_in.shape[0]
    offs = _in_offsets(d)

    def cols(names):
        w = jnp.concatenate([w_in[:, offs[n][0]:offs[n][0] + offs[n][1]] for n in names], axis=1)
        bias = jnp.concatenate([b_in[offs[n][0]:offs[n][0] + offs[n][1]] for n in names], axis=0)
        return w, bias

    normal = ['fox_k', 'fox_v', 'ml_q', 'ml_k', 'ml_v', 'ml_o', 'mem_q', 'g_fox', 'g_ml', 'g_mem']
    gates = ['fox_f', 'ml_i', 'ml_f']
    wn_p, bn_p = cols(normal)
    wn_s, bn_s = cols(normal + ['fox_q'])
    wt_p, bt_p = cols(gates + ['fox_q', 'fox_k', 'fox_v'])
    wt_s, bt_s = cols(gates)
    row = lambda a: a.reshape(1, -1)
    return dict(
        wn_p=wn_p.astype(BF16), bn_p=row(bn_p), wt_p=wt_p.T.astype(BF16), bt_p=bt_p.reshape(-1, 1),
        wn_s=wn_s.astype(BF16), bn_s=row(bn_s), wt_s=wt_s.T.astype(BF16), bt_s=bt_s.reshape(-1, 1),
        norm1_g=row(p['norm1_g'][l]), norm2_g=row(p['norm2_g'][l]),
        fox_knorm_g=row(jnp.tile(p['fox_knorm_g'][l], H_FOX)), fox_qnorm_g=row(jnp.tile(p['fox_qnorm_g'][l], H_FOX)),
        fox_qnorm_gc=p['fox_qnorm_g'][l].reshape(-1, 1), fox_knorm_gc=p['fox_knorm_g'][l].reshape(-1, 1),
        mem_qnorm_g=row(jnp.tile(p['mem_qnorm_g'][l], H_MEM)), mem_knorm_g=row(jnp.tile(p['mem_knorm_g'][l], H_MEM)),
        mem_norm_g=row(p['mem_norm_g'][l]), w_mem_kv=p['w_mem_kv'][l].astype(BF16),
        ml_hnorm_g=p['ml_hnorm_g'][l],
        w_br_fox=p['w_br_fox'][l].astype(BF16), w_br_ml=p['w_br_ml'][l].astype(BF16),
        w_br_mem=p['w_br_mem'][l].astype(BF16), w_out=p['w_out'][l].astype(BF16),
        w_up=p['w_up'][l].astype(BF16), w_down=p['w_down'][l].astype(BF16),
        conv_w=p['conv_w'][l], conv_b=row(p['conv_b'][l]))


def _gate_layouts(ct, b, steps, tm):
    i_row = ct[H_FOX:H_FOX + H_ML].reshape(H_ML, b, steps, tm).transpose(1, 2, 0, 3)
    b_row = ct[H_FOX + H_ML:].reshape(H_ML, b, steps, tm).transpose(1, 2, 0, 3)
    return i_row, b_row, i_row.transpose(0, 1, 3, 2), b_row.transpose(0, 1, 3, 2)


def _state_aug(c, n):
    return jnp.concatenate([c, n[..., None], jnp.zeros(c.shape[:-1] + (DH_ML - 1,), F32)], axis=-1)


def _layer_prompt(x, mem, lw):
    b, s, d = x.shape
    t = b * s
    tm = min(ROW_TILE, s)
    (fk16, mq, mk, mv, mo, memq, gates, gt, qT, vT, kT32, vT32) = _in_proj(
        x.reshape(t, d), lw, transposed_qv=True, tm=tm, seq=s)
    ct = _scan(gt, width=s, seg_fox=s, seg_ml=min(LANES, s), valid=None)
    n_hp = H_FOX // 2
    fcum = ct[:H_FOX]
    crow = fcum.reshape(n_hp, 2, t // tm, tm).transpose(2, 0, 1, 3)
    ccol = fcum.reshape(n_hp, 2, b, s).transpose(2, 0, 3, 1)
    fox = _fox_prompt(qT, fk16.reshape(b, s, FOX_W), vT.reshape(b, s // tm, FOX_W, tm), crow, ccol,
                      b=b, s=s, tq=tm, tk=tm)
    chunk = min(LANES, s)
    n_chunks = max(1, min(4, s // chunk))
    m_tm = chunk * n_chunks
    i_row, b_row, i_col, b_col = _gate_layouts(ct, b, s // m_tm, m_tm)
    r3 = lambda a: a.reshape(b, s, -1)
    c0 = jnp.zeros((b, H_ML, DH_ML, 2 * DH_ML), F32)
    m0 = jnp.zeros((b, H_ML, 1, 1), F32)
    ml, c_new, m_new = _mlstm(r3(mq), r3(mk), r3(mv), r3(mo), i_col, b_col, i_row, b_row, c0, m0,
                              lw['ml_hnorm_g'], chunk=chunk, n_chunks=n_chunks)
    nm = mem.shape[1]
    mk32, mv32 = _mem_kv(mem.reshape(b * nm, d), lw['mem_norm_g'], lw['w_mem_kv'], lw['mem_knorm_g'])
    mk32, mv32 = mk32.reshape(b, nm, MEM_W), mv32.reshape(b, nm, MEM_W)
    mem_out = _mem_attn(r3(memq), mk32, mv32, tm=tm)
    x1 = _merge(x.reshape(t, d), fox.reshape(t, FOX_W), ml.reshape(t, ML_W), mem_out.reshape(t, MEM_W), gates,
                lw, tm=tm)
    dff2 = lw['w_up'].shape[1]
    hist = jnp.zeros((b, CONV_W - 1, dff2), F32)
    y, conv_new = _ffn(x1.reshape(b, s, d), lw, hist, hist, seg=None, tm=tm)
    state = dict(
        fk=kT32.reshape(b, H_FOX, DH_FOX, s).transpose(0, 3, 1, 2),
        fv=vT32.reshape(b, H_FOX, DH_FOX, s).transpose(0, 3, 1, 2),
        fl=fcum_to_logf(gt, b, s),
        mc=c_new[..., :DH_ML], mn=c_new[..., DH_ML], mm=m_new.reshape(b, H_ML),
        mk=mk32.reshape(b, nm, H_MEM, DH_MEM), mv=mv32.reshape(b, nm, H_MEM, DH_MEM), cv=conv_new)
    return y, state


def fcum_to_logf(gt, b, s):
    return gt[:H_FOX].reshape(H_FOX, b, s).transpose(1, 2, 0)


def _layer_sample(layer, x, lw, page_table, kc, vc, lc, mem_k, mem_v, c_st, n_st, m_st, conv_st, n_valid):
    db, tp, d = x.shape
    t = db * tp
    (fk16, mq, mk, mv, mo, memq, gates, gt, fq, fk32, fv32) = _in_proj(
        x.reshape(t, d), lw, transposed_qv=False, tm=min(ROW_TILE, t))
    ct = _scan(gt, width=t, seg_fox=tp, seg_ml=tp, valid=n_valid)
    r3 = lambda a: a.reshape(db, tp, -1)
    fcum = ct[:H_FOX].reshape(H_FOX, db, tp)
    crow = fcum.transpose(1, 0, 2)
    fox = _fox_sample(layer, n_valid, page_table, r3(fq), r3(fk16), r3(fv32), crow,
                      crow.reshape(db, H_FOX * tp, 1), kc, vc, lc)
    i_row, b_row, i_col, b_col = _gate_layouts(ct, db, 1, tp)
    ml, c_new, m_new = _mlstm(r3(mq), r3(mk), r3(mv), r3(mo), i_col, b_col, i_row, b_row,
                              _state_aug(c_st, n_st), m_st.reshape(db, H_ML, 1, 1), lw['ml_hnorm_g'],
                              chunk=tp, n_chunks=1)
    nm = mem_k.shape[1]
    mem_out = _mem_attn(r3(memq), mem_k.reshape(db, nm, MEM_W), mem_v.reshape(db, nm, MEM_W), tm=tp)
    tm = min(2 * ROW_TILE, t)
    x1 = _merge(x.reshape(t, d), fox.reshape(t, FOX_W), ml.reshape(t, ML_W), mem_out.reshape(t, MEM_W), gates,
                lw, tm=tm)
    pad_rows = lambda a: jnp.pad(a, ((0, 0), (0, tp - 1), (0, 0))).reshape(t, -1)
    p0, p1 = pad_rows(conv_st[:, 0:1]), pad_rows(conv_st[:, 1:2])
    y, u = _ffn(x1.reshape(1, t, d), lw, p0, p1, seg=tp, tm=min(LANES, t))
    v4 = lambda a: a.reshape((db, tp) + a.shape[1:])[:, :n_valid]
    state = dict(
        fk=v4(fk32).reshape(db, n_valid, H_FOX, DH_FOX), fv=v4(fv32).reshape(db, n_valid, H_FOX, DH_FOX),
        fl=fcum_to_logf(gt, db, tp)[:, :n_valid],
        mc=c_new[..., :DH_ML], mn=c_new[..., DH_ML], mm=m_new.reshape(db, H_ML),
        cv=v4(u)[:, n_valid - (CONV_W - 1):])
    return y.reshape(db, tp, d), state


def kernel(x_prompt, x_sample, mem_prompt, cache_fox_k, cache_fox_v, cache_fox_logf, cache_mem_k, cache_mem_v,
           state_mlstm_C, state_mlstm_n, state_mlstm_m, state_ffn_conv, page_table, norm1_g, w_in, b_in,
           fox_qnorm_g, fox_knorm_g, ml_hnorm_g, mem_norm_g, w_mem_kv, mem_qnorm_g, mem_knorm_g, w_br_fox,
           w_br_ml, w_br_mem, w_out, norm2_g, w_up, conv_w, conv_b, w_down):
    params = dict(norm1_g=norm1_g, w_in=w_in, b_in=b_in, fox_qnorm_g=fox_qnorm_g, fox_knorm_g=fox_knorm_g,
                  ml_hnorm_g=ml_hnorm_g, mem_norm_g=mem_norm_g, w_mem_kv=w_mem_kv, mem_qnorm_g=mem_qnorm_g,
                  mem_knorm_g=mem_knorm_g, w_br_fox=w_br_fox, w_br_ml=w_br_ml, w_br_mem=w_br_mem, w_out=w_out,
                  norm2_g=norm2_g, w_up=w_up, conv_w=conv_w, conv_b=conv_b, w_down=w_down)
    depth, n_phys, page = cache_fox_logf.shape[:3]
    db, n_valid, d = x_sample.shape
    kc = jnp.transpose(cache_fox_k, (0, 1, 3, 4, 2))
    vc = jnp.transpose(cache_fox_v, (0, 1, 3, 4, 2))
    lf_t = jnp.swapaxes(cache_fox_logf, 2, 3).reshape(depth * n_phys * H_FOX, page)
    lc = _page_scan(lf_t).reshape(depth, n_phys, H_FOX, page)
    yp = x_prompt
    ys = jnp.pad(x_sample, ((0, 0), (0, SAMPLE_PAD - n_valid), (0, 0)))
    sp, ss = [], []
    for l in range(depth):
        lw = _layer_weights(l, params)
        yp, st = _layer_prompt(yp, mem_prompt, lw)
        sp.append(st)
        ys, st = _layer_sample(l, ys, lw, page_table, kc, vc, lc, cache_mem_k[l], cache_mem_v[l],
                               state_mlstm_C[l], state_mlstm_n[l], state_mlstm_m[l], state_ffn_conv[l], n_valid)
        ss.append(st)
    stack = lambda sts, key: jnp.stack([s[key] for s in sts])
    return (yp, ys[:, :n_valid],
            stack(sp, 'fk'), stack(sp, 'fv'), stack(sp, 'fl'), stack(ss, 'fk'), stack(ss, 'fv'), stack(ss, 'fl'),
            stack(sp, 'mc'), stack(sp, 'mn'), stack(sp, 'mm'), stack(ss, 'mc'), stack(ss, 'mn'), stack(ss, 'mm'),
            stack(sp, 'mk'), stack(sp, 'mv'), stack(sp, 'cv'), stack(ss, 'cv'))
```

```python
import functools

import jax
import jax.numpy as jnp
from jax import lax
from jax.experimental import pallas as pl
from jax.experimental.pallas import tpu as pltpu

F32 = jnp.float32
BF16 = jnp.bfloat16

H_FOX, DH_FOX = 8, 64
H_ML, DH_ML = 4, 128
H_MEM, DH_MEM = 4, 128
FOX_W = H_FOX * DH_FOX
ML_W = H_ML * DH_ML
MEM_W = H_MEM * DH_MEM
CONV_W = 3
EPS = 1e-6
NEG = -1e30
LOG2E = 1.4426950408889634
FOX_Q_SCALE = DH_FOX ** -0.5 * LOG2E

LANES = 128
V7X_VMEM_LIMIT = 56 * 1024 * 1024
SAMPLE_PAD = 16
ROW_TILE = 512
N_GATES = 16
PAGE_GROUP = 8
RING_GROUPS = 3


def _cparams(*sem):
    return pltpu.CompilerParams(dimension_semantics=sem, vmem_limit_bytes=V7X_VMEM_LIMIT)


def _resident(a):
    return pl.BlockSpec(a.shape, lambda *_: (0,) * a.ndim, pipeline_mode=pl.Buffered(1))


def _dot(a, b):
    return jnp.dot(a, b, preferred_element_type=F32)


def _dot_nt(a, b):
    return lax.dot_general(a, b, (((1,), (1,)), ((), ())), preferred_element_type=F32)


def _dot_tn(a, b):
    return lax.dot_general(a, b, (((0,), (0,)), ((), ())), preferred_element_type=F32)


def _split3(x):
    hi = x.astype(BF16).astype(F32)
    r = x - hi
    mid = r.astype(BF16).astype(F32)
    return [hi, mid, (r - mid).astype(BF16).astype(F32)]


def _log_sigmoid(x):
    return jnp.minimum(x, 0.0) - jnp.log1p(jnp.exp(-jnp.abs(x)))


def _sigmoid(x):
    return 1.0 / (1.0 + jnp.exp(-x))


def _rms(x):
    return x * lax.rsqrt(jnp.mean(x * x, axis=-1, keepdims=True) + EPS)


def _head_norm_lanes(z, dh):
    outs = []
    lane = lax.broadcasted_iota(jnp.int32, (1, LANES), 1)
    for j in range(z.shape[1] // LANES):
        x = z[:, j * LANES:(j + 1) * LANES]
        sq = x * x
        if dh == LANES:
            inv = lax.rsqrt(jnp.sum(sq, axis=-1, keepdims=True) * (1.0 / dh) + EPS)
        else:
            lo = lane < dh
            s_lo = jnp.sum(jnp.where(lo, sq, 0.0), axis=-1, keepdims=True)
            s_hi = jnp.sum(jnp.where(lo, 0.0, sq), axis=-1, keepdims=True)
            inv = jnp.where(lo, lax.rsqrt(s_lo * (1.0 / dh) + EPS), lax.rsqrt(s_hi * (1.0 / dh) + EPS))
        outs.append(x * inv)
    return jnp.concatenate(outs, axis=-1)


def _in_proj_kernel(x_ref, g1_ref, wn_ref, bn_ref, wt_ref, bt_ref, gk_ref, gkc_ref, gq_ref, gqc_ref, gmq_ref,
                    *out_refs, transposed_qv):
    if transposed_qv:
        (fk16_ref, mq_ref, mk_ref, mv_ref, mo_ref, memq_ref, gates_ref,
         gt_ref, qT_ref, vT_ref, kT32_ref, vT32_ref) = out_refs
    else:
        (fk16_ref, mq_ref, mk_ref, mv_ref, mo_ref, memq_ref, gates_ref,
         gt_ref, fq_ref, fk32_ref, fv32_ref) = out_refs
    h = (_rms(x_ref[...]) * g1_ref[...]).astype(BF16)

    def slab(i, width=FOX_W):
        return _dot(h, wn_ref[:, i:i + width]) + bn_ref[:, i:i + width]

    fk = _head_norm_lanes(slab(0), DH_FOX) * gk_ref[...]
    fk16_ref[...] = fk.astype(BF16)
    if not transposed_qv:
        fk32_ref[...] = fk
        fv32_ref[...] = slab(FOX_W)
    mq_ref[...] = slab(2 * FOX_W).astype(BF16)
    mk_ref[...] = (slab(3 * FOX_W) * DH_ML ** -0.5).astype(BF16)
    mv_ref[...] = slab(4 * FOX_W).astype(BF16)
    mo_ref[...] = _sigmoid(slab(5 * FOX_W)).astype(BF16)
    memq_ref[...] = (_head_norm_lanes(slab(6 * FOX_W), DH_MEM) * gmq_ref[...] * DH_MEM ** -0.5).astype(BF16)
    g0 = 7 * FOX_W
    for j in range(3):
        gates_ref[:, j * 1024:(j + 1) * 1024] = _sigmoid(slab(g0 + j * 1024, 1024)).astype(BF16)

    zt = _dot_nt(wt_ref[...], h) + bt_ref[...]
    zg = zt[0:N_GATES]
    row = lax.broadcasted_iota(jnp.int32, (N_GATES, 1), 0)
    is_ml_i = (row >= H_FOX) & (row < H_FOX + H_ML)
    gt_ref[...] = jnp.where(is_ml_i, zg, _log_sigmoid(zg))
    if transposed_qv:
        tm = zt.shape[1]

        def head_norm_rows(z, g_col):
            z = z.reshape(H_FOX, DH_FOX, tm)
            inv = lax.rsqrt(jnp.sum(z * z, axis=1, keepdims=True) * (1.0 / DH_FOX) + EPS)
            return (z * inv * g_col.reshape(1, DH_FOX, 1)).reshape(FOX_W, tm)

        qT_ref[0] = (head_norm_rows(zt[N_GATES:N_GATES + FOX_W], gqc_ref[...]) * FOX_Q_SCALE).astype(BF16)
        kT32_ref[0] = head_norm_rows(zt[N_GATES + FOX_W:N_GATES + 2 * FOX_W], gkc_ref[...])
        vT = zt[N_GATES + 2 * FOX_W:N_GATES + 3 * FOX_W]
        vT32_ref[0] = vT
        vT_ref[0] = vT.astype(BF16)
    else:
        fq = _head_norm_lanes(slab(g0 + 3 * 1024), DH_FOX) * gq_ref[...] * FOX_Q_SCALE
        fq_ref[...] = fq.astype(BF16)


def _in_proj(x, lw, *, transposed_qv, tm, seq=None):
    t, d = x.shape
    wn, bn, wt, bt = (lw['wn_p'], lw['bn_p'], lw['wt_p'], lw['bt_p']) if transposed_qv else (
        lw['wn_s'], lw['bn_s'], lw['wt_s'], lw['bt_s'])
    nt = t // tm
    rows = lambda i: (i, 0)

    def row_out(width, dtype):
        return jax.ShapeDtypeStruct((t, width), dtype), pl.BlockSpec((tm, width), rows)

    outs = [row_out(FOX_W, BF16), row_out(ML_W, BF16),
            row_out(ML_W, BF16), row_out(ML_W, BF16), row_out(ML_W, BF16), row_out(MEM_W, BF16),
            row_out(3 * d, BF16),
            (jax.ShapeDtypeStruct((N_GATES, t), F32), pl.BlockSpec((N_GATES, tm), lambda i: (0, i)))]
    if transposed_qv:
        per_seq = seq // tm
        blk = (jax.ShapeDtypeStruct((nt, FOX_W, tm), BF16), pl.BlockSpec((1, FOX_W, tm), lambda i: (i, 0, 0)))
        seq_minor = (jax.ShapeDtypeStruct((t // seq, FOX_W, seq), F32),
                     pl.BlockSpec((1, FOX_W, tm), lambda i: (i // per_seq, 0, i % per_seq)))
        outs += [blk, blk, seq_minor, seq_minor]
    else:
        outs += [row_out(FOX_W, BF16), row_out(FOX_W, F32), row_out(FOX_W, F32)]
    ins = [x, lw['norm1_g'], wn, bn, wt, bt, lw['fox_knorm_g'], lw['fox_knorm_gc'], lw['fox_qnorm_g'],
           lw['fox_qnorm_gc'], lw['mem_qnorm_g']]
    in_specs = [pl.BlockSpec((tm, d), rows)] + [_resident(a) for a in ins[1:]]
    return pl.pallas_call(
        functools.partial(_in_proj_kernel, transposed_qv=transposed_qv),
        out_shape=[o[0] for o in outs], grid=(nt,), in_specs=in_specs, out_specs=[o[1] for o in outs],
        compiler_params=_cparams("parallel"), name="in_proj")(*ins)


def _scan_kernel(g_ref, o_ref, *, seg_fox, seg_ml, valid):
    x = g_ref[...]
    w = x.shape[1]
    row = lax.broadcasted_iota(jnp.int32, (N_GATES, 1), 0)
    pos = lax.broadcasted_iota(jnp.int32, (1, w), 1)
    is_fox = row < H_FOX
    is_ml_i = (row >= H_FOX) & (row < H_FOX + H_ML)
    if valid is not None:
        pad = (pos % seg_ml) >= valid
        x = jnp.where(pad & is_ml_i, NEG, jnp.where(pad & jnp.logical_not(is_fox), 0.0, x))
    segpos = jnp.where(is_fox, pos % seg_fox, pos % seg_ml)
    sh = 1
    while sh < max(seg_fox, seg_ml):
        take = (segpos >= sh) & jnp.logical_not(is_ml_i)
        x = x + jnp.where(take, pltpu.roll(x, sh, axis=1), 0.0)
        sh *= 2
    o_ref[...] = x


def _scan(gt, *, width, seg_fox, seg_ml, valid):
    t = gt.shape[1]
    return pl.pallas_call(
        functools.partial(_scan_kernel, seg_fox=seg_fox, seg_ml=seg_ml, valid=valid),
        out_shape=jax.ShapeDtypeStruct(gt.shape, F32), grid=(t // width,),
        in_specs=[pl.BlockSpec((N_GATES, width), lambda i: (0, i))],
        out_specs=pl.BlockSpec((N_GATES, width), lambda i: (0, i)),
        compiler_params=_cparams("parallel"), name="gate_scan")(gt)


def _page_scan_kernel(x_ref, o_ref):
    x = x_ref[...]
    pos = lax.broadcasted_iota(jnp.int32, (1, x.shape[1]), 1)
    sh = 1
    while sh < x.shape[1]:
        x = x + jnp.where(pos >= sh, pltpu.roll(x, sh, axis=1), 0.0)
        sh *= 2
    o_ref[...] = x


def _page_scan(lf_t):
    rows, page = lf_t.shape
    tr = min(4096, rows)
    return pl.pallas_call(
        _page_scan_kernel, out_shape=jax.ShapeDtypeStruct(lf_t.shape, F32), grid=(rows // tr,),
        in_specs=[pl.BlockSpec((tr, page), lambda i: (i, 0))],
        out_specs=pl.BlockSpec((tr, page), lambda i: (i, 0)),
        compiler_params=_cparams("parallel"), name="page_scan")(lf_t)


def _fox_prompt_kernel(qT_ref, k_ref, vT_ref, crow_ref, ccol_ref, o_ref, aug_scr, rhs_scr, s_a, s_b, mx_a, mx_b,
                       p_scr, m_scr, l_scr, al_scr, acc_scr, *, tq, tk):
    assert tq == tk
    qi = pl.program_id(2)
    n_feat = 2 * DH_FOX

    @pl.when(qi == 0)
    def _():
        lane = lax.broadcasted_iota(jnp.int32, (1, n_feat), 1)

        tr = min(tk, 2 * LANES)

        def fill(i, carry):
            r0 = pl.multiple_of(i * tr, tr)
            c = ccol_ref[0, 0, pl.ds(r0, tr), :] * (-LOG2E)
            feats = [jnp.ones((tr, 1), F32)] * 3 + _split3(c[:, 0:1]) + _split3(c[:, 1:2])
            a = jnp.zeros((tr, n_feat), F32)
            for j, f in enumerate(feats):
                a = jnp.where(lane == j, f, a)
            aug_scr[pl.ds(r0, tr), :] = a.astype(BF16)
            return carry

        lax.fori_loop(0, k_ref.shape[1] // tr, fill, 0)

    qT2 = qT_ref[0]
    rows = lax.broadcasted_iota(jnp.int32, (n_feat, 1), 0)
    cq = crow_ref[0, 0] * LOG2E
    for hh in range(2):
        q_own = jnp.where((rows >= hh * DH_FOX) & (rows < (hh + 1) * DH_FOX), qT2, jnp.zeros_like(qT2))
        hi, mid, lo = _split3(cq[hh:hh + 1, :])
        ones_rows = (rows >= 3 + 3 * hh) & (rows < 6 + 3 * hh)
        feat = jnp.where(rows == 0, hi, jnp.where(rows == 1, mid, jnp.where(rows == 2, lo,
                         jnp.where(ones_rows, 1.0, 0.0))))
        rhs_scr[hh] = jnp.concatenate([q_own, feat.astype(BF16)], axis=0)
    m_scr[...] = jnp.full(m_scr.shape, NEG, F32)
    l_scr[...] = jnp.zeros(l_scr.shape, F32)
    al_scr[...] = jnp.ones(al_scr.shape, F32)
    acc_scr[...] = jnp.zeros(acc_scr.shape, F32)
    p_scr[...] = jnp.zeros(p_scr.shape, BF16)

    n_col = tq // LANES

    def put_cols(ref, hh, x):
        for c in range(n_col):
            ref[hh, c] = x[:, c * LANES:(c + 1) * LANES]

    def get_cols(ref, hh):
        return jnp.concatenate([ref[hh, c] for c in range(n_col)], axis=1)

    def scores(kb, s_ref, mx_ref):
        k0 = pl.multiple_of(kb * tk, tk)
        lhs = jnp.concatenate([k_ref[0, pl.ds(k0, tk), :], aug_scr[pl.ds(k0, tk), :]], axis=1)
        for hh in range(2):
            s = _dot(lhs, rhs_scr[hh])
            put_cols(s_ref, hh, s)
            mx_ref[hh] = jnp.max(s, axis=0, keepdims=True)

    def values(kb):
        vT2 = vT_ref[0, kb]
        for hh in range(2):
            acc_scr[hh] = (al_scr[hh] * acc_scr[hh]
                           + _dot(vT2[hh * DH_FOX:(hh + 1) * DH_FOX, :], get_cols(p_scr, hh)))

    def softmax(hh, s, mx):
        m_old = m_scr[hh]
        m_new = jnp.maximum(m_old, mx)
        alpha = jnp.exp2(m_old - m_new)
        p = jnp.exp2(s - m_new)
        l_scr[hh] = alpha * l_scr[hh] + jnp.sum(p, axis=0, keepdims=True)
        put_cols(p_scr, hh, p.astype(BF16))
        m_scr[hh] = m_new
        al_scr[hh] = alpha

    def phase(j, s_cur, mx_cur, s_nxt, mx_nxt):
        values(jnp.maximum(j - 1, 0))
        scores(j + 1, s_nxt, mx_nxt)
        for hh in range(2):
            softmax(hh, get_cols(s_cur, hh), mx_cur[hh])

    scores(0, s_a, mx_a)

    def trip(t, carry):
        j = 2 * t
        phase(j, s_a, mx_a, s_b, mx_b)

        @pl.when(j + 1 < qi)
        def _():
            phase(j + 1, s_b, mx_b, s_a, mx_a)

        return carry

    lax.fori_loop(0, (qi + 1) // 2, trip, 0)

    def finish(s_ref):
        values(jnp.maximum(qi - 1, 0))
        krow = qi * tk + lax.broadcasted_iota(jnp.int32, (tk, 1), 0)
        qcol = qi * tq + lax.broadcasted_iota(jnp.int32, (1, tq), 1)
        for hh in range(2):
            s = jnp.where(krow <= qcol, get_cols(s_ref, hh), -jnp.inf)
            softmax(hh, s, jnp.max(s, axis=0, keepdims=True))
        values(qi)
        outT = jnp.concatenate([acc_scr[0] / l_scr[0], acc_scr[1] / l_scr[1]], axis=0)
        o_ref[0] = outT.T.astype(BF16)

    @pl.when(qi % 2 == 0)
    def _():
        finish(s_a)

    @pl.when(qi % 2 == 1)
    def _():
        finish(s_b)


def _fox_prompt(qT, k16, vT, crow, ccol, *, b, s, tq, tk):
    nq, nk = s // tq, s // tk
    n_hp = H_FOX // 2
    pair = lambda shape, dtype: pltpu.VMEM((2,) + shape, dtype)
    slabs = (tq // LANES, tk, LANES)
    return pl.pallas_call(
        functools.partial(_fox_prompt_kernel, tq=tq, tk=tk),
        scratch_shapes=[pltpu.VMEM((s, 2 * DH_FOX), BF16), pair((4 * DH_FOX, tq), BF16),
                        pair(slabs, F32), pair(slabs, F32), pair((1, tq), F32), pair((1, tq), F32),
                        pair(slabs, BF16), pair((1, tq), F32), pair((1, tq), F32), pair((1, tq), F32),
                        pair((DH_FOX, tq), F32)],
        out_shape=jax.ShapeDtypeStruct((b, s, FOX_W), BF16), grid=(b, n_hp, nq),
        in_specs=[
            pl.BlockSpec((1, 2 * DH_FOX, tq), lambda bi, hp, qi: (bi * nq + qi, hp, 0)),
            pl.BlockSpec((1, s, 2 * DH_FOX), lambda bi, hp, qi: (bi, 0, hp)),
            pl.BlockSpec((1, nk, 2 * DH_FOX, tk), lambda bi, hp, qi: (bi, 0, hp, 0)),
            pl.BlockSpec((1, 1, 2, tq), lambda bi, hp, qi: (bi * nq + qi, hp, 0, 0)),
            pl.BlockSpec((1, 1, s, 2), lambda bi, hp, qi: (bi, hp, 0, 0)),
        ],
        out_specs=pl.BlockSpec((1, tq, 2 * DH_FOX), lambda bi, hp, qi: (bi, qi, hp)),
        compiler_params=_cparams("parallel", "parallel", "arbitrary"), name="fox_prompt")(qT, k16, vT, crow, ccol)


def _fox_sample_kernel(pt_ref, q_ref, kn_ref, vn_ref, crow_ref, ccol_ref, kc_ref, vc_ref, lc_ref, o_ref,
                       kbuf, vbuf, lbuf, sem, m_scr, l_scr, acc_scr, *, layer, n_seq, n_pages, page, n_valid, group, ring):
    b = pl.program_id(0)
    nb = pl.num_programs(0)
    tp = q_ref.shape[1]
    rows = H_FOX * tp

    def rep_heads(a):
        return jnp.broadcast_to(a[:, None, :], (H_FOX, tp, a.shape[-1])).reshape(rows, a.shape[-1])

    n_groups = n_pages // group

    def group_copies(bi, gi, rslot):
        out = []
        for g in range(group):
            pid = pt_ref[bi, n_pages - 1 - (gi * group + g)]
            slot = rslot * group + g
            out += [pltpu.make_async_copy(kc_ref.at[layer, pid], kbuf.at[slot], sem.at[0, slot]),
                    pltpu.make_async_copy(vc_ref.at[layer, pid], vbuf.at[slot], sem.at[1, slot]),
                    pltpu.make_async_copy(lc_ref.at[layer, pid], lbuf.at[slot], sem.at[2, slot])]
        return out

    @pl.when(b == 0)
    def _():
        for d in range(min(ring - 1, n_seq * n_groups)):
            for c in group_copies(d // n_groups, d % n_groups, d):
                c.start()

    row_id = lax.broadcasted_iota(jnp.int32, (rows, 1), 0)
    own = (row_id // tp) == (lax.broadcasted_iota(jnp.int32, (1, FOX_W), 1) // DH_FOX)
    q = q_ref[0]
    q_bd = jnp.where(own, jnp.broadcast_to(q[None], (H_FOX, tp, FOX_W)).reshape(rows, FOX_W), jnp.zeros((), BF16))
    cc = ccol_ref[0] * LOG2E
    spos = lax.broadcasted_iota(jnp.int32, (1, tp), 1)
    new_ok = (spos <= (row_id % tp)) & (spos < n_valid)
    s = _dot_nt(q_bd, kn_ref[0]) + (cc - rep_heads(crow_ref[0] * LOG2E))
    s = jnp.where(new_ok, s, -jnp.inf)
    m = jnp.max(s, axis=1, keepdims=True)
    p = jnp.exp2(s - m)
    m_scr[...] = m
    l_scr[...] = jnp.sum(p, axis=1, keepdims=True)
    acc_scr[...] = _dot(p.astype(BF16), vn_ref[0].astype(BF16))

    def step(i, suffix):
        gidx = b * n_groups + i
        rslot = gidx % ring
        for c in group_copies(b, i, rslot):
            c.wait()
        nxt = gidx + (ring - 1)
        nxt_b = nxt // n_groups

        @pl.when(nxt_b < nb)
        def _():
            for c in group_copies(nxt_b, nxt % n_groups, nxt % ring):
                c.start()

        after, k_pages, v_pages = [], [], []
        for g in range(group):
            slot = rslot * group + g
            lcum = lbuf[slot]
            tot = lcum[:, page - 1:page]
            after.append(suffix + tot - lcum)
            suffix = suffix + tot
            k_pages.append(kbuf[slot].reshape(FOX_W, page).astype(BF16))
            v_pages.append(vbuf[slot].reshape(FOX_W, page).astype(BF16))
        bias = cc + rep_heads(jnp.concatenate(after, axis=1) * LOG2E)
        s = _dot(q_bd, jnp.concatenate(k_pages, axis=1)) + bias
        m_old = m_scr[...]
        m_new = jnp.maximum(m_old, jnp.max(s, axis=1, keepdims=True))
        alpha = jnp.exp2(m_old - m_new)
        p = jnp.exp2(s - m_new)
        m_scr[...] = m_new
        l_scr[...] = alpha * l_scr[...] + jnp.sum(p, axis=1, keepdims=True)
        acc_scr[...] = alpha * acc_scr[...] + _dot_nt(p.astype(BF16), jnp.concatenate(v_pages, axis=1))
        return suffix

    lax.fori_loop(0, n_groups, step, jnp.zeros((H_FOX, 1), F32))
    out = jnp.where(own, acc_scr[...] / l_scr[...], 0.0)
    o_ref[0] = sum(out[h * tp:(h + 1) * tp] for h in range(H_FOX)).astype(BF16)


def _fox_sample(layer, n_valid, page_table, q16, kn16, vn32, crow, ccol, kc, vc, lc):
    db, n_pages = page_table.shape
    tp = q16.shape[1]
    page = lc.shape[3]
    rows = H_FOX * tp
    group = max(g for g in range(1, PAGE_GROUP + 1) if n_pages % g == 0)
    slots = group * RING_GROUPS
    per_seq = lambda bi, pt: (bi, 0, 0)
    grid_spec = pltpu.PrefetchScalarGridSpec(
        num_scalar_prefetch=1, grid=(db,),
        in_specs=[pl.BlockSpec((1, tp, FOX_W), per_seq), pl.BlockSpec((1, tp, FOX_W), per_seq),
                  pl.BlockSpec((1, tp, FOX_W), per_seq), pl.BlockSpec((1, H_FOX, tp), per_seq),
                  pl.BlockSpec((1, rows, 1), per_seq),
                  pl.BlockSpec(memory_space=pl.ANY), pl.BlockSpec(memory_space=pl.ANY),
                  pl.BlockSpec(memory_space=pl.ANY)],
        out_specs=pl.BlockSpec((1, tp, FOX_W), per_seq),
        scratch_shapes=[pltpu.VMEM((slots, H_FOX, DH_FOX, page), F32), pltpu.VMEM((slots, H_FOX, DH_FOX, page), F32),
                        pltpu.VMEM((slots, H_FOX, page), F32), pltpu.SemaphoreType.DMA((3, slots)),
                        pltpu.VMEM((rows, 1), F32), pltpu.VMEM((rows, 1), F32), pltpu.VMEM((rows, FOX_W), F32)])
    return pl.pallas_call(
        functools.partial(_fox_sample_kernel, layer=layer, n_seq=db, n_pages=n_pages, page=page, n_valid=n_valid,
                          group=group, ring=RING_GROUPS),
        out_shape=jax.ShapeDtypeStruct((db, tp, FOX_W), BF16), grid_spec=grid_spec,
        compiler_params=_cparams("arbitrary"), name="fox_sample")(page_table, q16, kn16, vn32, crow, ccol, kc, vc, lc)


def _mlstm_kernel(q_ref, k_ref, v_ref, o_ref, icol_ref, bcol_ref, irow_ref, brow_ref, c0_ref, m0_ref, g_ref,
                  out_ref, cn_ref, mn_ref, c_scr, m_scr, *, chunk, n_chunks):
    st = pl.program_id(1)

    @pl.when(st == 0)
    def _():
        c_scr[...] = c0_ref[0]
        m_scr[...] = m0_ref[0]

    lane = lax.broadcasted_iota(jnp.int32, (chunk, DH_ML), 1)
    ones_col = jnp.where(lane == 0, 1.0, 0.0).astype(BF16)
    tpos = lax.broadcasted_iota(jnp.int32, (chunk, 1), 0)
    spos = lax.broadcasted_iota(jnp.int32, (1, chunk), 1)
    causal = tpos >= spos
    heads = range(H_ML)
    cs = [slice(h * DH_ML, (h + 1) * DH_ML) for h in heads]
    each = lambda f: [f(h) for h in heads]
    pre = []
    for c in range(n_chunks):
        rs = slice(c * chunk, (c + 1) * chunk)
        bc = each(lambda h: bcol_ref[0, 0, rs, h:h + 1])
        dlog = each(lambda h: jnp.where(causal, bc[h] - brow_ref[0, 0, h:h + 1, rs] + irow_ref[0, 0, h:h + 1, rs],
                                        -jnp.inf))
        dmax = each(lambda h: jnp.max(dlog[h], axis=1, keepdims=True))
        qk = each(lambda h: _dot_nt(q_ref[0, rs, cs[h]], k_ref[0, rs, cs[h]]))
        pre.append((bc, dlog, dmax, qk))
    m_st = each(lambda h: m_scr[h])
    c_aug = each(lambda h: c_scr[h])
    for c in range(n_chunks):
        rs = slice(c * chunk, (c + 1) * chunk)
        bc, dlog, dmax, qk = pre[c]
        inter = each(lambda h: bc[h] + m_st[h])
        m_t = each(lambda h: jnp.maximum(inter[h], dmax[h]))
        w_intra = each(lambda h: jnp.exp(dlog[h] - m_t[h]))
        w_inter = each(lambda h: jnp.exp(inter[h] - m_t[h]))
        sc = each(lambda h: (qk[h] * w_intra[h]).astype(BF16))
        v_aug = each(lambda h: jnp.concatenate([v_ref[0, rs, cs[h]], ones_col], axis=1))
        numden = each(lambda h: _dot(sc[h], v_aug[h])
                      + w_inter[h] * _dot(q_ref[0, rs, cs[h]], c_aug[h].astype(BF16)))
        hid = each(lambda h: numden[h][:, :DH_ML]
                   / jnp.maximum(jnp.abs(numden[h][:, DH_ML:DH_ML + 1]), jnp.exp(-m_t[h])))
        m_new = each(lambda h: m_t[h][chunk - 1:chunk, :])
        b_last = each(lambda h: bc[h][chunk - 1:chunk, :])
        w_last = each(lambda h: jnp.exp(b_last[h] - bc[h] + icol_ref[0, 0, rs, h:h + 1] - m_new[h]))
        decay = each(lambda h: jnp.exp(b_last[h] + m_st[h] - m_new[h]))
        kw = each(lambda h: (k_ref[0, rs, cs[h]].astype(F32) * w_last[h]).astype(BF16))
        c_aug = each(lambda h: decay[h] * c_aug[h] + _dot_tn(kw[h], v_aug[h]))
        m_st = m_new
        normed = each(lambda h: _rms(hid[h]) * g_ref[h:h + 1, :])
        for h in heads:
            out_ref[0, rs, cs[h]] = (normed[h] * o_ref[0, rs, cs[h]].astype(F32)).astype(BF16)
    for h in heads:
        c_scr[h] = c_aug[h]
        m_scr[h] = m_st[h]
    cn_ref[0] = c_scr[...]
    mn_ref[0] = m_scr[...]


def _mlstm(q, k, v, o, icol, bcol, irow, brow, c0, m0, g, *, chunk, n_chunks):
    b, rows, _ = q.shape
    tm = chunk * n_chunks
    steps = rows // tm
    tok = pl.BlockSpec((1, tm, ML_W), lambda bi, si: (bi, si, 0))
    col = pl.BlockSpec((1, 1, tm, H_ML), lambda bi, si: (bi, si, 0, 0))
    rowspec = pl.BlockSpec((1, 1, H_ML, tm), lambda bi, si: (bi, si, 0, 0))
    cspec = pl.BlockSpec((1, H_ML, DH_ML, 2 * DH_ML), lambda bi, si: (bi, 0, 0, 0))
    mspec = pl.BlockSpec((1, H_ML, 1, 1), lambda bi, si: (bi, 0, 0, 0))
    return pl.pallas_call(
        functools.partial(_mlstm_kernel, chunk=chunk, n_chunks=n_chunks),
        out_shape=[jax.ShapeDtypeStruct((b, rows, ML_W), BF16), jax.ShapeDtypeStruct(c0.shape, F32),
                   jax.ShapeDtypeStruct(m0.shape, F32)],
        grid=(b, steps),
        in_specs=[tok, tok, tok, tok, col, col, rowspec, rowspec, cspec, mspec,
                  _resident(g)],
        out_specs=[tok, cspec, mspec],
        scratch_shapes=[pltpu.VMEM((H_ML, DH_ML, 2 * DH_ML), F32), pltpu.VMEM((H_ML, 1, 1), F32)],
        compiler_params=_cparams("parallel", "arbitrary"), name="mlstm")(q, k, v, o, icol, bcol, irow, brow, c0, m0, g)


def _mem_kv_kernel(mem_ref, g_ref, w_ref, gk_ref, k_ref, v_ref):
    h = (_rms(mem_ref[...]) * g_ref[...]).astype(BF16)
    kv = _dot(h, w_ref[...])
    k_ref[...] = _head_norm_lanes(kv[:, :MEM_W], DH_MEM) * gk_ref[...]
    v_ref[...] = kv[:, MEM_W:]


def _mem_kv(mem, g, w16, gk):
    rows, d = mem.shape
    tm = 256
    return pl.pallas_call(
        _mem_kv_kernel,
        out_shape=[jax.ShapeDtypeStruct((rows, MEM_W), F32)] * 2, grid=(rows // tm,),
        in_specs=[pl.BlockSpec((tm, d), lambda i: (i, 0)), _resident(g), _resident(w16), _resident(gk)],
        out_specs=[pl.BlockSpec((tm, MEM_W), lambda i: (i, 0))] * 2,
        compiler_params=_cparams("parallel"), name="mem_kv")(mem, g, w16, gk)


def _mem_attn_kernel(q_ref, k_ref, v_ref, o_ref):
    q = q_ref[0]
    k = k_ref[0].astype(BF16)
    v = v_ref[0].astype(BF16)
    cs = [slice(h * DH_MEM, (h + 1) * DH_MEM) for h in range(H_MEM)]
    s = [_dot_nt(q[:, c], k[:, c]) for c in cs]
    mx = [jnp.max(x, axis=-1, keepdims=True) for x in s]
    p = [jnp.exp(x - m) for x, m in zip(s, mx)]
    l = [jnp.sum(x, axis=-1, keepdims=True) for x in p]
    outs = [_dot(x.astype(BF16), v[:, c]) / d for x, c, d in zip(p, cs, l)]
    o_ref[0] = jnp.concatenate(outs, axis=-1).astype(BF16)


def _mem_attn(q, mk, mv, *, tm):
    b, rows, _ = q.shape
    nm = mk.shape[1]
    tok = pl.BlockSpec((1, tm, MEM_W), lambda bi, si: (bi, si, 0))
    mem = pl.BlockSpec((1, nm, MEM_W), lambda bi, si: (bi, 0, 0))
    return pl.pallas_call(
        _mem_attn_kernel, out_shape=jax.ShapeDtypeStruct(q.shape, BF16), grid=(b, rows // tm),
        in_specs=[tok, mem, mem], out_specs=tok,
        compiler_params=_cparams("parallel", "arbitrary"), name="mem_attn")(q, mk, mv)


def _merge_kernel(x_ref, fox_ref, ml_ref, mem_ref, gates_ref, wf_ref, wm_ref, wx_ref, wo_ref, y_ref):
    d = x_ref.shape[1]
    merged = (gates_ref[:, 0:d].astype(F32) * _dot(fox_ref[...], wf_ref[...])
              + gates_ref[:, d:2 * d].astype(F32) * _dot(ml_ref[...], wm_ref[...])
              + gates_ref[:, 2 * d:3 * d].astype(F32) * _dot(mem_ref[...], wx_ref[...]))
    y_ref[...] = x_ref[...] + _dot(merged.astype(BF16), wo_ref[...])


def _merge(x, fox, ml, mem, gates, lw, *, tm):
    t, d = x.shape
    rows = lambda i: (i, 0)
    ws = [lw['w_br_fox'], lw['w_br_ml'], lw['w_br_mem'], lw['w_out']]
    return pl.pallas_call(
        _merge_kernel, out_shape=jax.ShapeDtypeStruct((t, d), F32), grid=(t // tm,),
        in_specs=[pl.BlockSpec((tm, d), rows), pl.BlockSpec((tm, FOX_W), rows), pl.BlockSpec((tm, ML_W), rows),
                  pl.BlockSpec((tm, MEM_W), rows), pl.BlockSpec((tm, 3 * d), rows)]
                 + [_resident(w) for w in ws],
        out_specs=pl.BlockSpec((tm, d), rows),
        compiler_params=_cparams("parallel"), name="merge")(x, fox, ml, mem, gates, *ws)


def _ffn_kernel(x_ref, g_ref, wu_ref, cw_ref, cb_ref, wd_ref, p0_ref, p1_ref, y_ref, aux_ref, carry,
                *, seg, n_split):
    tm = x_ref.shape[-2]
    dff2 = wu_ref.shape[1]
    dff = dff2 // 2
    cw = dff // n_split
    x = x_ref[0]
    h = (_rms(x) * g_ref[...]).astype(BF16)
    row = lax.broadcasted_iota(jnp.int32, (tm, 1), 0)
    if seg is None:
        @pl.when(pl.program_id(1) == 0)
        def _():
            carry[...] = p0_ref[0]
        first, second = row == 0, row == 1
    else:
        first, second = (row % seg) == 0, (row % seg) == 1

    def conv(cols):
        u = _dot(h, wu_ref[:, cols])
        if seg is None:
            hist0, hist1 = carry[0:1, cols], carry[1:2, cols]
            prev1 = jnp.where(first, hist1, pltpu.roll(u, 1, axis=0))
            prev2 = jnp.where(first, hist0, jnp.where(second, hist1, pltpu.roll(u, 2, axis=0)))
            carry[:, cols] = u[tm - 2:tm, :]
        else:
            prev1 = jnp.where(first, p1_ref[:, cols], pltpu.roll(u, 1, axis=0))
            prev2 = jnp.where(first, p0_ref[:, cols],
                              jnp.where(second, pltpu.roll(p1_ref[:, cols], 1, axis=0), pltpu.roll(u, 2, axis=0)))
            aux_ref[:, cols] = u
        return cb_ref[:, cols] + cw_ref[0:1, cols] * prev2 + cw_ref[1:2, cols] * prev1 + cw_ref[2:3, cols] * u

    acc = x
    for j in range(n_split):
        gate = conv(slice(j * cw, (j + 1) * cw))
        val = conv(slice(dff + j * cw, dff + (j + 1) * cw))
        act = (gate * _sigmoid(gate) * val).astype(BF16)
        acc = acc + _dot(act, wd_ref[j * cw:(j + 1) * cw, :])
    y_ref[0] = acc
    if seg is None:
        aux_ref[0] = carry[...]


def _ffn(x, lw, p0, p1, *, seg, tm):
    b, rows, d = x.shape
    dff2 = lw['w_up'].shape[1]
    tok = pl.BlockSpec((1, tm, d), lambda bi, si: (bi, si, 0))
    if seg is None:
        hist = pl.BlockSpec((1, CONV_W - 1, dff2), lambda bi, si: (bi, 0, 0))
        p_specs = [hist, hist]
        aux_shape, aux_spec = jax.ShapeDtypeStruct((b, CONV_W - 1, dff2), F32), hist
    else:
        full = pl.BlockSpec((tm, dff2), lambda bi, si: (si, 0))
        p_specs = [full, full]
        aux_shape, aux_spec = jax.ShapeDtypeStruct((rows, dff2), F32), full
    ws = [lw['norm2_g'], lw['w_up'], lw['conv_w'], lw['conv_b'], lw['w_down']]
    return pl.pallas_call(
        functools.partial(_ffn_kernel, seg=seg, n_split=2),
        out_shape=[jax.ShapeDtypeStruct(x.shape, F32), aux_shape], grid=(b, rows // tm),
        in_specs=[tok] + [_resident(w) for w in ws] + p_specs,
        out_specs=[tok, aux_spec],
        scratch_shapes=[pltpu.VMEM((CONV_W - 1, dff2), F32)],
        compiler_params=_cparams("parallel", "arbitrary"), name="ffn")(x, *ws, p0, p1)


def _in_offsets(d):
    sizes = (('fox_q', FOX_W), ('fox_k', FOX_W), ('fox_v', FOX_W), ('fox_f', H_FOX),
             ('ml_q', ML_W), ('ml_k', ML_W), ('ml_v', ML_W), ('ml_i', H_ML), ('ml_f', H_ML), ('ml_o', ML_W),
             ('mem_q', MEM_W), ('g_fox', d), ('g_ml', d), ('g_mem', d))
    offs, start = {}, 0
    for name, size in sizes:
        offs[name] = (start, size)
        start += size
    return offs


def _layer_weights(l, p):
    w_in, b_in = p['w_in'][l], p['b_in'][l]
    d = w_in.shape[0]
    offs = _in_offsets(d)

    def cols(names):
        w = jnp.concatenate([w_in[:, offs[n][0]:offs[n][0] + offs[n][1]] for n in names], axis=1)
        bias = jnp.concatenate([b_in[offs[n][0]:offs[n][0] + offs[n][1]] for n in names], axis=0)
        return w, bias

    normal = ['fox_k', 'fox_v', 'ml_q', 'ml_k', 'ml_v', 'ml_o', 'mem_q', 'g_fox', 'g_ml', 'g_mem']
    gates = ['fox_f', 'ml_i', 'ml_f']
    wn_p, bn_p = cols(normal)
    wn_s, bn_s = cols(normal + ['fox_q'])
    wt_p, bt_p = cols(gates + ['fox_q', 'fox_k', 'fox_v'])
    wt_s, bt_s = cols(gates)
    row = lambda a: a.reshape(1, -1)
    return dict(
        wn_p=wn_p.astype(BF16), bn_p=row(bn_p), wt_p=wt_p.T.astype(BF16), bt_p=bt_p.reshape(-1, 1),
        wn_s=wn_s.astype(BF16), bn_s=row(bn_s), wt_s=wt_s.T.astype(BF16), bt_s=bt_s.reshape(-1, 1),
        norm1_g=row(p['norm1_g'][l]), norm2_g=row(p['norm2_g'][l]),
        fox_knorm_g=row(jnp.tile(p['fox_knorm_g'][l], H_FOX)), fox_qnorm_g=row(jnp.tile(p['fox_qnorm_g'][l], H_FOX)),
        fox_qnorm_gc=p['fox_qnorm_g'][l].reshape(-1, 1), fox_knorm_gc=p['fox_knorm_g'][l].reshape(-1, 1),
        mem_qnorm_g=row(jnp.tile(p['mem_qnorm_g'][l], H_MEM)), mem_knorm_g=row(jnp.tile(p['mem_knorm_g'][l], H_MEM)),
        mem_norm_g=row(p['mem_norm_g'][l]), w_mem_kv=p['w_mem_kv'][l].astype(BF16),
        ml_hnorm_g=p['ml_hnorm_g'][l],
        w_br_fox=p['w_br_fox'][l].astype(BF16), w_br_ml=p['w_br_ml'][l].astype(BF16),
        w_br_mem=p['w_br_mem'][l].astype(BF16), w_out=p['w_out'][l].astype(BF16),
        w_up=p['w_up'][l].astype(BF16), w_down=p['w_down'][l].astype(BF16),
        conv_w=p['conv_w'][l], conv_b=row(p['conv_b'][l]))


def _gate_layouts(ct, b, steps, tm):
    i_row = ct[H_FOX:H_FOX + H_ML].reshape(H_ML, b, steps, tm).transpose(1, 2, 0, 3)
    b_row = ct[H_FOX + H_ML:].reshape(H_ML, b, steps, tm).transpose(1, 2, 0, 3)
    return i_row, b_row, i_row.transpose(0, 1, 3, 2), b_row.transpose(0, 1, 3, 2)


def _state_aug(c, n):
    return jnp.concatenate([c, n[..., None], jnp.zeros(c.shape[:-1] + (DH_ML - 1,), F32)], axis=-1)


def _layer_prompt(x, mem, lw):
    b, s, d = x.shape
    t = b * s
    tm = min(ROW_TILE, s)
    (fk16, mq, mk, mv, mo, memq, gates, gt, qT, vT, kT32, vT32) = _in_proj(
        x.reshape(t, d), lw, transposed_qv=True, tm=tm, seq=s)
    ct = _scan(gt, width=s, seg_fox=s, seg_ml=min(LANES, s), valid=None)
    n_hp = H_FOX // 2
    fcum = ct[:H_FOX]
    crow = fcum.reshape(n_hp, 2, t // tm, tm).transpose(2, 0, 1, 3)
    ccol = fcum.reshape(n_hp, 2, b, s).transpose(2, 0, 3, 1)
    fox = _fox_prompt(qT, fk16.reshape(b, s, FOX_W), vT.reshape(b, s // tm, FOX_W, tm), crow, ccol,
                      b=b, s=s, tq=tm, tk=tm)
    chunk = min(LANES, s)
    n_chunks = max(1, min(4, s // chunk))
    m_tm = chunk * n_chunks
    i_row, b_row, i_col, b_col = _gate_layouts(ct, b, s // m_tm, m_tm)
    r3 = lambda a: a.reshape(b, s, -1)
    c0 = jnp.zeros((b, H_ML, DH_ML, 2 * DH_ML), F32)
    m0 = jnp.zeros((b, H_ML, 1, 1), F32)
    ml, c_new, m_new = _mlstm(r3(mq), r3(mk), r3(mv), r3(mo), i_col, b_col, i_row, b_row, c0, m0,
                              lw['ml_hnorm_g'], chunk=chunk, n_chunks=n_chunks)
    nm = mem.shape[1]
    mk32, mv32 = _mem_kv(mem.reshape(b * nm, d), lw['mem_norm_g'], lw['w_mem_kv'], lw['mem_knorm_g'])
    mk32, mv32 = mk32.reshape(b, nm, MEM_W), mv32.reshape(b, nm, MEM_W)
    mem_out = _mem_attn(r3(memq), mk32, mv32, tm=tm)
    x1 = _merge(x.reshape(t, d), fox.reshape(t, FOX_W), ml.reshape(t, ML_W), mem_out.reshape(t, MEM_W), gates,
                lw, tm=tm)
    dff2 = lw['w_up'].shape[1]
    hist = jnp.zeros((b, CONV_W - 1, dff2), F32)
    y, conv_new = _ffn(x1.reshape(b, s, d), lw, hist, hist, seg=None, tm=tm)
    state = dict(
        fk=kT32.reshape(b, H_FOX, DH_FOX, s).transpose(0, 3, 1, 2),
        fv=vT32.reshape(b, H_FOX, DH_FOX, s).transpose(0, 3, 1, 2),
        fl=fcum_to_logf(gt, b, s),
        mc=c_new[..., :DH_ML], mn=c_new[..., DH_ML], mm=m_new.reshape(b, H_ML),
        mk=mk32.reshape(b, nm, H_MEM, DH_MEM), mv=mv32.reshape(b, nm, H_MEM, DH_MEM), cv=conv_new)
    return y, state


def fcum_to_logf(gt, b, s):
    return gt[:H_FOX].reshape(H_FOX, b, s).transpose(1, 2, 0)


def _layer_sample(layer, x, lw, page_table, kc, vc, lc, mem_k, mem_v, c_st, n_st, m_st, conv_st, n_valid):
    db, tp, d = x.shape
    t = db * tp
    (fk16, mq, mk, mv, mo, memq, gates, gt, fq, fk32, fv32) = _in_proj(
        x.reshape(t, d), lw, transposed_qv=False, tm=min(ROW_TILE, t))
    ct = _scan(gt, width=t, seg_fox=tp, seg_ml=tp, valid=n_valid)
    r3 = lambda a: a.reshape(db, tp, -1)
    fcum = ct[:H_FOX].reshape(H_FOX, db, tp)
    crow = fcum.transpose(1, 0, 2)
    fox = _fox_sample(layer, n_valid, page_table, r3(fq), r3(fk16), r3(fv32), crow,
                      crow.reshape(db, H_FOX * tp, 1), kc, vc, lc)
    i_row, b_row, i_col, b_col = _gate_layouts(ct, db, 1, tp)
    ml, c_new, m_new = _mlstm(r3(mq), r3(mk), r3(mv), r3(mo), i_col, b_col, i_row, b_row,
                              _state_aug(c_st, n_st), m_st.reshape(db, H_ML, 1, 1), lw['ml_hnorm_g'],
                              chunk=tp, n_chunks=1)
    nm = mem_k.shape[1]
    mem_out = _mem_attn(r3(memq), mem_k.reshape(db, nm, MEM_W), mem_v.reshape(db, nm, MEM_W), tm=tp)
    tm = min(2 * ROW_TILE, t)
    x1 = _merge(x.reshape(t, d), fox.reshape(t, FOX_W), ml.reshape(t, ML_W), mem_out.reshape(t, MEM_W), gates,
                lw, tm=tm)
    pad_rows = lambda a: jnp.pad(a, ((0, 0), (0, tp - 1), (0, 0))).reshape(t, -1)
    p0, p1 = pad_rows(conv_st[:, 0:1]), pad_rows(conv_st[:, 1:2])
    y, u = _ffn(x1.reshape(1, t, d), lw, p0, p1, seg=tp, tm=min(LANES, t))
    v4 = lambda a: a.reshape((db, tp) + a.shape[1:])[:, :n_valid]
    state = dict(
        fk=v4(fk32).reshape(db, n_valid, H_FOX, DH_FOX), fv=v4(fv32).reshape(db, n_valid, H_FOX, DH_FOX),
        fl=fcum_to_logf(gt, db, tp)[:, :n_valid],
        mc=c_new[..., :DH_ML], mn=c_new[..., DH_ML], mm=m_new.reshape(db, H_ML),
        cv=v4(u)[:, n_valid - (CONV_W - 1):])
    return y.reshape(db, tp, d), state


def kernel(x_prompt, x_sample, mem_prompt, cache_fox_k, cache_fox_v, cache_fox_logf, cache_mem_k, cache_mem_v,
           state_mlstm_C, state_mlstm_n, state_mlstm_m, state_ffn_conv, page_table, norm1_g, w_in, b_in,
           fox_qnorm_g, fox_knorm_g, ml_hnorm_g, mem_norm_g, w_mem_kv, mem_qnorm_g, mem_knorm_g, w_br_fox,
           w_br_ml, w_br_mem, w_out, norm2_g, w_up, conv_w, conv_b, w_down):
    params = dict(norm1_g=norm1_g, w_in=w_in, b_in=b_in, fox_qnorm_g=fox_qnorm_g, fox_knorm_g=fox_knorm_g,
                  ml_hnorm_g=ml_hnorm_g, mem_norm_g=mem_norm_g, w_mem_kv=w_mem_kv, mem_qnorm_g=mem_qnorm_g,
                  mem_knorm_g=mem_knorm_g, w_br_fox=w_br_fox, w_br_ml=w_br_ml, w_br_mem=w_br_mem, w_out=w_out,
                  norm2_g=norm2_g, w_up=w_up, conv_w=conv_w, conv_b=conv_b, w_down=w_down)
    depth, n_phys, page = cache_fox_logf.shape[:3]
    db, n_valid, d = x_sample.shape
    kc = jnp.transpose(cache_fox_k, (0, 1, 3, 4, 2))
    vc = jnp.transpose(cache_fox_v, (0, 1, 3, 4, 2))
    lf_t = jnp.swapaxes(cache_fox_logf, 2, 3).reshape(depth * n_phys * H_FOX, page)
    lc = _page_scan(lf_t).reshape(depth, n_phys, H_FOX, page)
    yp = x_prompt
    ys = jnp.pad(x_sample, ((0, 0), (0, SAMPLE_PAD - n_valid), (0, 0)))
    sp, ss = [], []
    for l in range(depth):
        lw = _layer_weights(l, params)
        yp, st = _layer_prompt(yp, mem_prompt, lw)
        sp.append(st)
        ys, st = _layer_sample(l, ys, lw, page_table, kc, vc, lc, cache_mem_k[l], cache_mem_v[l],
                               state_mlstm_C[l], state_mlstm_n[l], state_mlstm_m[l], state_ffn_conv[l], n_valid)
        ss.append(st)
    stack = lambda sts, key: jnp.stack([s[key] for s in sts])
    return (yp, ys[:, :n_valid],
            stack(sp, 'fk'), stack(sp, 'fv'), stack(sp, 'fl'), stack(ss, 'fk'), stack(ss, 'fv'), stack(ss, 'fl'),
            stack(sp, 'mc'), stack(sp, 'mn'), stack(sp, 'mm'), stack(ss, 'mc'), stack(ss, 'mn'), stack(ss, 'mm'),
            stack(sp, 'mk'), stack(sp, 'mv'), stack(sp, 'cv'), stack(ss, 'cv'))
```

```python
import functools

import jax
import jax.numpy as jnp
from jax import lax
from jax.experimental import pallas as pl
from jax.experimental.pallas import tpu as pltpu

F32 = jnp.float32
BF16 = jnp.bfloat16

H_FOX, DH_FOX = 8, 64
H_ML, DH_ML = 4, 128
H_MEM, DH_MEM = 4, 128
FOX_W = H_FOX * DH_FOX
ML_W = H_ML * DH_ML
MEM_W = H_MEM * DH_MEM
CONV_W = 3
EPS = 1e-6
NEG = -1e30
LOG2E = 1.4426950408889634
FOX_Q_SCALE = DH_FOX ** -0.5 * LOG2E

LANES = 128
V7X_VMEM_LIMIT = 56 * 1024 * 1024
SAMPLE_PAD = 16
ROW_TILE = 512
N_GATES = 16
PAGE_GROUP = 8
RING_GROUPS = 3


def _cparams(*sem):
    return pltpu.CompilerParams(dimension_semantics=sem, vmem_limit_bytes=V7X_VMEM_LIMIT)


def _resident(a):
    return pl.BlockSpec(a.shape, lambda *_: (0,) * a.ndim, pipeline_mode=pl.Buffered(1))


def _dot(a, b):
    return jnp.dot(a, b, preferred_element_type=F32)


def _dot_nt(a, b):
    return lax.dot_general(a, b, (((1,), (1,)), ((), ())), preferred_element_type=F32)


def _dot_tn(a, b):
    return lax.dot_general(a, b, (((0,), (0,)), ((), ())), preferred_element_type=F32)


def _split3(x):
    hi = x.astype(BF16).astype(F32)
    r = x - hi
    mid = r.astype(BF16).astype(F32)
    return [hi, mid, (r - mid).astype(BF16).astype(F32)]


def _log_sigmoid(x):
    return jnp.minimum(x, 0.0) - jnp.log1p(jnp.exp(-jnp.abs(x)))


def _sigmoid(x):
    return 1.0 / (1.0 + jnp.exp(-x))


def _rms(x):
    return x * lax.rsqrt(jnp.mean(x * x, axis=-1, keepdims=True) + EPS)


def _head_norm_lanes(z, dh):
    outs = []
    lane = lax.broadcasted_iota(jnp.int32, (1, LANES), 1)
    for j in range(z.shape[1] // LANES):
        x = z[:, j * LANES:(j + 1) * LANES]
        sq = x * x
        if dh == LANES:
            inv = lax.rsqrt(jnp.sum(sq, axis=-1, keepdims=True) * (1.0 / dh) + EPS)
        else:
            lo = lane < dh
            s_lo = jnp.sum(jnp.where(lo, sq, 0.0), axis=-1, keepdims=True)
            s_hi = jnp.sum(jnp.where(lo, 0.0, sq), axis=-1, keepdims=True)
            inv = jnp.where(lo, lax.rsqrt(s_lo * (1.0 / dh) + EPS), lax.rsqrt(s_hi * (1.0 / dh) + EPS))
        outs.append(x * inv)
    return jnp.concatenate(outs, axis=-1)


def _in_proj_kernel(x_ref, g1_ref, wn_ref, bn_ref, wt_ref, bt_ref, gk_ref, gkc_ref, gq_ref, gqc_ref, gmq_ref,
                    *out_refs, transposed_qv):
    if transposed_qv:
        (fk16_ref, mq_ref, mk_ref, mv_ref, mo_ref, memq_ref, gates_ref,
         gt_ref, qT_ref, vT_ref, kT32_ref, vT32_ref) = out_refs
    else:
        (fk16_ref, mq_ref, mk_ref, mv_ref, mo_ref, memq_ref, gates_ref,
         gt_ref, fq_ref, fk32_ref, fv32_ref) = out_refs
    h = (_rms(x_ref[...]) * g1_ref[...]).astype(BF16)

    def slab(i, width=FOX_W):
        return _dot(h, wn_ref[:, i:i + width]) + bn_ref[:, i:i + width]

    fk = _head_norm_lanes(slab(0), DH_FOX) * gk_ref[...]
    fk16_ref[...] = fk.astype(BF16)
    if not transposed_qv:
        fk32_ref[...] = fk
        fv32_ref[...] = slab(FOX_W)
    mq_ref[...] = slab(2 * FOX_W).astype(BF16)
    mk_ref[...] = (slab(3 * FOX_W) * DH_ML ** -0.5).astype(BF16)
    mv_ref[...] = slab(4 * FOX_W).astype(BF16)
    mo_ref[...] = _sigmoid(slab(5 * FOX_W)).astype(BF16)
    memq_ref[...] = (_head_norm_lanes(slab(6 * FOX_W), DH_MEM) * gmq_ref[...] * DH_MEM ** -0.5).astype(BF16)
    g0 = 7 * FOX_W
    for j in range(3):
        gates_ref[:, j * 1024:(j + 1) * 1024] = _sigmoid(slab(g0 + j * 1024, 1024)).astype(BF16)

    zt = _dot_nt(wt_ref[...], h) + bt_ref[...]
    zg = zt[0:N_GATES]
    row = lax.broadcasted_iota(jnp.int32, (N_GATES, 1), 0)
    is_ml_i = (row >= H_FOX) & (row < H_FOX + H_ML)
    gt_ref[...] = jnp.where(is_ml_i, zg, _log_sigmoid(zg))
    if transposed_qv:
        tm = zt.shape[1]

        def head_norm_rows(z, g_col):
            z = z.reshape(H_FOX, DH_FOX, tm)
            inv = lax.rsqrt(jnp.sum(z * z, axis=1, keepdims=True) * (1.0 / DH_FOX) + EPS)
            return (z * inv * g_col.reshape(1, DH_FOX, 1)).reshape(FOX_W, tm)

        qT_ref[0] = (head_norm_rows(zt[N_GATES:N_GATES + FOX_W], gqc_ref[...]) * FOX_Q_SCALE).astype(BF16)
        kT32_ref[0] = head_norm_rows(zt[N_GATES + FOX_W:N_GATES + 2 * FOX_W], gkc_ref[...])
        vT = zt[N_GATES + 2 * FOX_W:N_GATES + 3 * FOX_W]
        vT32_ref[0] = vT
        vT_ref[0] = vT.astype(BF16)
    else:
        fq = _head_norm_lanes(slab(g0 + 3 * 1024), DH_FOX) * gq_ref[...] * FOX_Q_SCALE
        fq_ref[...] = fq.astype(BF16)


def _in_proj(x, lw, *, transposed_qv, tm, seq=None):
    t, d = x.shape
    wn, bn, wt, bt = (lw['wn_p'], lw['bn_p'], lw['wt_p'], lw['bt_p']) if transposed_qv else (
        lw['wn_s'], lw['bn_s'], lw['wt_s'], lw['bt_s'])
    nt = t // tm
    rows = lambda i: (i, 0)

    def row_out(width, dtype):
        return jax.ShapeDtypeStruct((t, width), dtype), pl.BlockSpec((tm, width), rows)

    outs = [row_out(FOX_W, BF16), row_out(ML_W, BF16),
            row_out(ML_W, BF16), row_out(ML_W, BF16), row_out(ML_W, BF16), row_out(MEM_W, BF16),
            row_out(3 * d, BF16),
            (jax.ShapeDtypeStruct((N_GATES, t), F32), pl.BlockSpec((N_GATES, tm), lambda i: (0, i)))]
    if transposed_qv:
        per_seq = seq // tm
        blk = (jax.ShapeDtypeStruct((nt, FOX_W, tm), BF16), pl.BlockSpec((1, FOX_W, tm), lambda i: (i, 0, 0)))
        seq_minor = (jax.ShapeDtypeStruct((t // seq, FOX_W, seq), F32),
                     pl.BlockSpec((1, FOX_W, tm), lambda i: (i // per_seq, 0, i % per_seq)))
        outs += [blk, blk, seq_minor, seq_minor]
    else:
        outs += [row_out(FOX_W, BF16), row_out(FOX_W, F32), row_out(FOX_W, F32)]
    ins = [x, lw['norm1_g'], wn, bn, wt, bt, lw['fox_knorm_g'], lw['fox_knorm_gc'], lw['fox_qnorm_g'],
           lw['fox_qnorm_gc'], lw['mem_qnorm_g']]
    in_specs = [pl.BlockSpec((tm, d), rows)] + [_resident(a) for a in ins[1:]]
    return pl.pallas_call(
        functools.partial(_in_proj_kernel, transposed_qv=transposed_qv),
        out_shape=[o[0] for o in outs], grid=(nt,), in_specs=in_specs, out_specs=[o[1] for o in outs],
        compiler_params=_cparams("parallel"), name="in_proj")(*ins)


def _scan_kernel(g_ref, o_ref, *, seg_fox, seg_ml, valid):
    x = g_ref[...]
    w = x.shape[1]
    row = lax.broadcasted_iota(jnp.int32, (N_GATES, 1), 0)
    pos = lax.broadcasted_iota(jnp.int32, (1, w), 1)
    is_fox = row < H_FOX
    is_ml_i = (row >= H_FOX) & (row < H_FOX + H_ML)
    if valid is not None:
        pad = (pos % seg_ml) >= valid
        x = jnp.where(pad & is_ml_i, NEG, jnp.where(pad & jnp.logical_not(is_fox), 0.0, x))
    segpos = jnp.where(is_fox, pos % seg_fox, pos % seg_ml)
    sh = 1
    while sh < max(seg_fox, seg_ml):
        take = (segpos >= sh) & jnp.logical_not(is_ml_i)
        x = x + jnp.where(take, pltpu.roll(x, sh, axis=1), 0.0)
        sh *= 2
    o_ref[...] = x


def _scan(gt, *, width, seg_fox, seg_ml, valid):
    t = gt.shape[1]
    return pl.pallas_call(
        functools.partial(_scan_kernel, seg_fox=seg_fox, seg_ml=seg_ml, valid=valid),
        out_shape=jax.ShapeDtypeStruct(gt.shape, F32), grid=(t // width,),
        in_specs=[pl.BlockSpec((N_GATES, width), lambda i: (0, i))],
        out_specs=pl.BlockSpec((N_GATES, width), lambda i: (0, i)),
        compiler_params=_cparams("parallel"), name="gate_scan")(gt)


def _page_scan_kernel(x_ref, o_ref):
    x = x_ref[...]
    pos = lax.broadcasted_iota(jnp.int32, (1, x.shape[1]), 1)
    sh = 1
    while sh < x.shape[1]:
        x = x + jnp.where(pos >= sh, pltpu.roll(x, sh, axis=1), 0.0)
        sh *= 2
    o_ref[...] = x


def _page_scan(lf_t):
    rows, page = lf_t.shape
    tr = min(4096, rows)
    return pl.pallas_call(
        _page_scan_kernel, out_shape=jax.ShapeDtypeStruct(lf_t.shape, F32), grid=(rows // tr,),
        in_specs=[pl.BlockSpec((tr, page), lambda i: (i, 0))],
        out_specs=pl.BlockSpec((tr, page), lambda i: (i, 0)),
        compiler_params=_cparams("parallel"), name="page_scan")(lf_t)


def _fox_prompt_kernel(qT_ref, k_ref, vT_ref, crow_ref, ckey_ref, o_ref, aug_scr, rhs_scr, s_a, s_b, mx_a, mx_b,
                       p_scr, m_scr, l_scr, al_scr, acc_scr, *, tq, tk):
    assert tq == tk
    qi = pl.program_id(2)
    n_feat = 2 * DH_FOX

    @pl.when(qi == 0)
    def _():
        n_chunk, _, tr = ckey_ref.shape[2:]
        feat_row = lax.broadcasted_iota(jnp.int32, (n_feat, 1), 0)

        def fill(i, carry):
            c = ckey_ref[0, 0, i] * (-LOG2E)
            feats = [jnp.ones((1, tr), F32)] * 3 + _split3(c[0:1, :]) + _split3(c[1:2, :])
            a = jnp.zeros((n_feat, tr), F32)
            for j, f in enumerate(feats):
                a = jnp.where(feat_row == j, f, a)
            aug_scr[pl.ds(pl.multiple_of(i * tr, tr), tr), :] = a.T.astype(BF16)
            return carry

        lax.fori_loop(0, n_chunk, fill, 0)

    qT2 = qT_ref[0]
    rows = lax.broadcasted_iota(jnp.int32, (n_feat, 1), 0)
    cq = crow_ref[0, 0] * LOG2E
    for hh in range(2):
        q_own = jnp.where((rows >= hh * DH_FOX) & (rows < (hh + 1) * DH_FOX), qT2, jnp.zeros_like(qT2))
        hi, mid, lo = _split3(cq[hh:hh + 1, :])
        ones_rows = (rows >= 3 + 3 * hh) & (rows < 6 + 3 * hh)
        feat = jnp.where(rows == 0, hi, jnp.where(rows == 1, mid, jnp.where(rows == 2, lo,
                         jnp.where(ones_rows, 1.0, 0.0))))
        rhs_scr[hh] = jnp.concatenate([q_own, feat.astype(BF16)], axis=0)
    m_scr[...] = jnp.full(m_scr.shape, NEG, F32)
    l_scr[...] = jnp.zeros(l_scr.shape, F32)
    al_scr[...] = jnp.ones(al_scr.shape, F32)
    acc_scr[...] = jnp.zeros(acc_scr.shape, F32)
    p_scr[...] = jnp.zeros(p_scr.shape, BF16)

    n_col = tq // LANES

    def put_cols(ref, hh, x):
        for c in range(n_col):
            ref[hh, c] = x[:, c * LANES:(c + 1) * LANES]

    def get_cols(ref, hh):
        return jnp.concatenate([ref[hh, c] for c in range(n_col)], axis=1)

    def scores(kb, s_ref, mx_ref):
        k0 = pl.multiple_of(kb * tk, tk)
        lhs = jnp.concatenate([k_ref[0, pl.ds(k0, tk), :], aug_scr[pl.ds(k0, tk), :]], axis=1)
        for hh in range(2):
            s = _dot(lhs, rhs_scr[hh])
            put_cols(s_ref, hh, s)
            mx_ref[hh] = jnp.max(s, axis=0, keepdims=True)

    def values(kb):
        vT2 = vT_ref[0, kb]
        for hh in range(2):
            acc_scr[hh] = (al_scr[hh] * acc_scr[hh]
                           + _dot(vT2[hh * DH_FOX:(hh + 1) * DH_FOX, :], get_cols(p_scr, hh)))

    def softmax(hh, s, mx):
        m_old = m_scr[hh]
        m_new = jnp.maximum(m_old, mx)
        alpha = jnp.exp2(m_old - m_new)
        p = jnp.exp2(s - m_new)
        l_scr[hh] = alpha * l_scr[hh] + jnp.sum(p, axis=0, keepdims=True)
        put_cols(p_scr, hh, p.astype(BF16))
        m_scr[hh] = m_new
        al_scr[hh] = alpha

    def phase(j, s_cur, mx_cur, s_nxt, mx_nxt):
        values(jnp.maximum(j - 1, 0))
        scores(j + 1, s_nxt, mx_nxt)
        for hh in range(2):
            softmax(hh, get_cols(s_cur, hh), mx_cur[hh])

    scores(0, s_a, mx_a)

    def trip(t, carry):
        j = 2 * t
        phase(j, s_a, mx_a, s_b, mx_b)

        @pl.when(j + 1 < qi)
        def _():
            phase(j + 1, s_b, mx_b, s_a, mx_a)

        return carry

    lax.fori_loop(0, (qi + 1) // 2, trip, 0)

    def finish(s_ref):
        values(jnp.maximum(qi - 1, 0))
        krow = qi * tk + lax.broadcasted_iota(jnp.int32, (tk, 1), 0)
        qcol = qi * tq + lax.broadcasted_iota(jnp.int32, (1, tq), 1)
        for hh in range(2):
            s = jnp.where(krow <= qcol, get_cols(s_ref, hh), -jnp.inf)
            softmax(hh, s, jnp.max(s, axis=0, keepdims=True))
        values(qi)
        outT = jnp.concatenate([acc_scr[0] / l_scr[0], acc_scr[1] / l_scr[1]], axis=0)
        o_ref[0] = outT.T.astype(BF16)

    @pl.when(qi % 2 == 0)
    def _():
        finish(s_a)

    @pl.when(qi % 2 == 1)
    def _():
        finish(s_b)


def _fox_prompt(qT, k16, vT, crow, ckey, *, b, s, tq, tk):
    nq, nk = s // tq, s // tk
    n_hp = H_FOX // 2
    pair = lambda shape, dtype: pltpu.VMEM((2,) + shape, dtype)
    slabs = (tq // LANES, tk, LANES)
    return pl.pallas_call(
        functools.partial(_fox_prompt_kernel, tq=tq, tk=tk),
        scratch_shapes=[pltpu.VMEM((s, 2 * DH_FOX), BF16), pair((4 * DH_FOX, tq), BF16),
                        pair(slabs, F32), pair(slabs, F32), pair((1, tq), F32), pair((1, tq), F32),
                        pair(slabs, BF16), pair((1, tq), F32), pair((1, tq), F32), pair((1, tq), F32),
                        pair((DH_FOX, tq), F32)],
        out_shape=jax.ShapeDtypeStruct((b, s, FOX_W), BF16), grid=(b, n_hp, nq),
        in_specs=[
            pl.BlockSpec((1, 2 * DH_FOX, tq), lambda bi, hp, qi: (bi * nq + qi, hp, 0)),
            pl.BlockSpec((1, s, 2 * DH_FOX), lambda bi, hp, qi: (bi, 0, hp)),
            pl.BlockSpec((1, nk, 2 * DH_FOX, tk), lambda bi, hp, qi: (bi, 0, hp, 0)),
            pl.BlockSpec((1, 1, 2, tq), lambda bi, hp, qi: (bi * nq + qi, hp, 0, 0)),
            pl.BlockSpec((1, 1) + ckey.shape[2:], lambda bi, hp, qi: (bi, hp, 0, 0, 0)),
        ],
        out_specs=pl.BlockSpec((1, tq, 2 * DH_FOX), lambda bi, hp, qi: (bi, qi, hp)),
        compiler_params=_cparams("parallel", "parallel", "arbitrary"), name="fox_prompt")(qT, k16, vT, crow, ckey)


def _fox_sample_kernel(pt_ref, q_ref, kn_ref, vn_ref, crow_ref, ccol_ref, kc_ref, vc_ref, lc_ref, o_ref,
                       kbuf, vbuf, lbuf, sem, m_scr, l_scr, acc_scr, *, layer, n_seq, n_pages, page, n_valid, group, ring):
    b = pl.program_id(0)
    nb = pl.num_programs(0)
    tp = q_ref.shape[1]
    rows = H_FOX * tp

    def rep_heads(a):
        return jnp.broadcast_to(a[:, None, :], (H_FOX, tp, a.shape[-1])).reshape(rows, a.shape[-1])

    n_groups = n_pages // group

    def group_copies(bi, gi, rslot):
        out = []
        for g in range(group):
            pid = pt_ref[bi, n_pages - 1 - (gi * group + g)]
            slot = rslot * group + g
            out += [pltpu.make_async_copy(kc_ref.at[layer, pid], kbuf.at[slot], sem.at[0, slot]),
                    pltpu.make_async_copy(vc_ref.at[layer, pid], vbuf.at[slot], sem.at[1, slot]),
                    pltpu.make_async_copy(lc_ref.at[layer, pid], lbuf.at[slot], sem.at[2, slot])]
        return out

    @pl.when(b == 0)
    def _():
        for d in range(min(ring - 1, n_seq * n_groups)):
            for c in group_copies(d // n_groups, d % n_groups, d):
                c.start()

    row_id = lax.broadcasted_iota(jnp.int32, (rows, 1), 0)
    own = (row_id // tp) == (lax.broadcasted_iota(jnp.int32, (1, FOX_W), 1) // DH_FOX)
    q = q_ref[0]
    q_bd = jnp.where(own, jnp.broadcast_to(q[None], (H_FOX, tp, FOX_W)).reshape(rows, FOX_W), jnp.zeros((), BF16))
    cc = ccol_ref[0] * LOG2E
    spos = lax.broadcasted_iota(jnp.int32, (1, tp), 1)
    new_ok = (spos <= (row_id % tp)) & (spos < n_valid)
    s = _dot_nt(q_bd, kn_ref[0]) + (cc - rep_heads(crow_ref[0] * LOG2E))
    s = jnp.where(new_ok, s, -jnp.inf)
    m = jnp.max(s, axis=1, keepdims=True)
    p = jnp.exp2(s - m)
    m_scr[...] = m
    l_scr[...] = jnp.sum(p, axis=1, keepdims=True)
    acc_scr[...] = _dot(p.astype(BF16), vn_ref[0].astype(BF16))

    def step(i, suffix):
        gidx = b * n_groups + i
        rslot = gidx % ring
        for c in group_copies(b, i, rslot):
            c.wait()
        nxt = gidx + (ring - 1)
        nxt_b = nxt // n_groups

        @pl.when(nxt_b < nb)
        def _():
            for c in group_copies(nxt_b, nxt % n_groups, nxt % ring):
                c.start()

        after, k_pages, v_pages = [], [], []
        for g in range(group):
            slot = rslot * group + g
            lcum = lbuf[slot]
            tot = lcum[:, page - 1:page]
            after.append(suffix + tot - lcum)
            suffix = suffix + tot
            k_pages.append(kbuf[slot].reshape(FOX_W, page).astype(BF16))
            v_pages.append(vbuf[slot].reshape(FOX_W, page).astype(BF16))
        bias = cc + rep_heads(jnp.concatenate(after, axis=1) * LOG2E)
        s = _dot(q_bd, jnp.concatenate(k_pages, axis=1)) + bias
        m_old = m_scr[...]
        m_new = jnp.maximum(m_old, jnp.max(s, axis=1, keepdims=True))
        alpha = jnp.exp2(m_old - m_new)
        p = jnp.exp2(s - m_new)
        m_scr[...] = m_new
        l_scr[...] = alpha * l_scr[...] + jnp.sum(p, axis=1, keepdims=True)
        acc_scr[...] = alpha * acc_scr[...] + _dot_nt(p.astype(BF16), jnp.concatenate(v_pages, axis=1))
        return suffix

    lax.fori_loop(0, n_groups, step, jnp.zeros((H_FOX, 1), F32))
    out = jnp.where(own, acc_scr[...] / l_scr[...], 0.0)
    o_ref[0] = sum(out[h * tp:(h + 1) * tp] for h in range(H_FOX)).astype(BF16)


def _fox_sample(layer, n_valid, page_table, q16, kn16, vn32, crow, ccol, kc, vc, lc):
    db, n_pages = page_table.shape
    tp = q16.shape[1]
    page = lc.shape[3]
    rows = H_FOX * tp
    group = max(g for g in range(1, PAGE_GROUP + 1) if n_pages % g == 0)
    slots = group * RING_GROUPS
    per_seq = lambda bi, pt: (bi, 0, 0)
    grid_spec = pltpu.PrefetchScalarGridSpec(
        num_scalar_prefetch=1, grid=(db,),
        in_specs=[pl.BlockSpec((1, tp, FOX_W), per_seq), pl.BlockSpec((1, tp, FOX_W), per_seq),
                  pl.BlockSpec((1, tp, FOX_W), per_seq), pl.BlockSpec((1, H_FOX, tp), per_seq),
                  pl.BlockSpec((1, rows, 1), per_seq),
                  pl.BlockSpec(memory_space=pl.ANY), pl.BlockSpec(memory_space=pl.ANY),
                  pl.BlockSpec(memory_space=pl.ANY)],
        out_specs=pl.BlockSpec((1, tp, FOX_W), per_seq),
        scratch_shapes=[pltpu.VMEM((slots, H_FOX, DH_FOX, page), F32), pltpu.VMEM((slots, H_FOX, DH_FOX, page), F32),
                        pltpu.VMEM((slots, H_FOX, page), F32), pltpu.SemaphoreType.DMA((3, slots)),
                        pltpu.VMEM((rows, 1), F32), pltpu.VMEM((rows, 1), F32), pltpu.VMEM((rows, FOX_W), F32)])
    return pl.pallas_call(
        functools.partial(_fox_sample_kernel, layer=layer, n_seq=db, n_pages=n_pages, page=page, n_valid=n_valid,
                          group=group, ring=RING_GROUPS),
        out_shape=jax.ShapeDtypeStruct((db, tp, FOX_W), BF16), grid_spec=grid_spec,
        compiler_params=_cparams("arbitrary"), name="fox_sample")(page_table, q16, kn16, vn32, crow, ccol, kc, vc, lc)


def _mlstm_kernel(q_ref, k_ref, v_ref, o_ref, icol_ref, bcol_ref, irow_ref, brow_ref, c0_ref, m0_ref, g_ref,
                  out_ref, cn_ref, mn_ref, c_scr, m_scr, *, chunk, n_chunks):
    st = pl.program_id(1)

    @pl.when(st == 0)
    def _():
        c_scr[...] = c0_ref[0]
        m_scr[...] = m0_ref[0]

    lane = lax.broadcasted_iota(jnp.int32, (chunk, DH_ML), 1)
    ones_col = jnp.where(lane == 0, 1.0, 0.0).astype(BF16)
    tpos = lax.broadcasted_iota(jnp.int32, (chunk, 1), 0)
    spos = lax.broadcasted_iota(jnp.int32, (1, chunk), 1)
    causal = tpos >= spos
    heads = range(H_ML)
    cs = [slice(h * DH_ML, (h + 1) * DH_ML) for h in heads]
    each = lambda f: [f(h) for h in heads]
    pre = []
    for c in range(n_chunks):
        rs = slice(c * chunk, (c + 1) * chunk)
        bc = each(lambda h: bcol_ref[0, 0, rs, h:h + 1])
        dlog = each(lambda h: jnp.where(causal, bc[h] - brow_ref[0, 0, h:h + 1, rs] + irow_ref[0, 0, h:h + 1, rs],
                                        -jnp.inf))
        dmax = each(lambda h: jnp.max(dlog[h], axis=1, keepdims=True))
        qk = each(lambda h: _dot_nt(q_ref[0, rs, cs[h]], k_ref[0, rs, cs[h]]))
        pre.append((bc, dlog, dmax, qk))
    m_st = each(lambda h: m_scr[h])
    c_aug = each(lambda h: c_scr[h])
    for c in range(n_chunks):
        rs = slice(c * chunk, (c + 1) * chunk)
        bc, dlog, dmax, qk = pre[c]
        inter = each(lambda h: bc[h] + m_st[h])
        m_t = each(lambda h: jnp.maximum(inter[h], dmax[h]))
        w_intra = each(lambda h: jnp.exp(dlog[h] - m_t[h]))
        w_inter = each(lambda h: jnp.exp(inter[h] - m_t[h]))
        sc = each(lambda h: (qk[h] * w_intra[h]).astype(BF16))
        v_aug = each(lambda h: jnp.concatenate([v_ref[0, rs, cs[h]], ones_col], axis=1))
        numden = each(lambda h: _dot(sc[h], v_aug[h])
                      + w_inter[h] * _dot(q_ref[0, rs, cs[h]], c_aug[h].astype(BF16)))
        hid = each(lambda h: numden[h][:, :DH_ML]
                   / jnp.maximum(jnp.abs(numden[h][:, DH_ML:DH_ML + 1]), jnp.exp(-m_t[h])))
        m_new = each(lambda h: m_t[h][chunk - 1:chunk, :])
        b_last = each(lambda h: bc[h][chunk - 1:chunk, :])
        w_last = each(lambda h: jnp.exp(b_last[h] - bc[h] + icol_ref[0, 0, rs, h:h + 1] - m_new[h]))
        decay = each(lambda h: jnp.exp(b_last[h] + m_st[h] - m_new[h]))
        kw = each(lambda h: (k_ref[0, rs, cs[h]].astype(F32) * w_last[h]).astype(BF16))
        c_aug = each(lambda h: decay[h] * c_aug[h] + _dot_tn(kw[h], v_aug[h]))
        m_st = m_new
        normed = each(lambda h: _rms(hid[h]) * g_ref[h:h + 1, :])
        for h in heads:
            out_ref[0, rs, cs[h]] = (normed[h] * o_ref[0, rs, cs[h]].astype(F32)).astype(BF16)
    for h in heads:
        c_scr[h] = c_aug[h]
        m_scr[h] = m_st[h]
    cn_ref[0] = c_scr[...]
    mn_ref[0] = m_scr[...]


def _mlstm(q, k, v, o, icol, bcol, irow, brow, c0, m0, g, *, chunk, n_chunks):
    b, rows, _ = q.shape
    tm = chunk * n_chunks
    steps = rows // tm
    tok = pl.BlockSpec((1, tm, ML_W), lambda bi, si: (bi, si, 0))
    col = pl.BlockSpec((1, 1, tm, H_ML), lambda bi, si: (bi, si, 0, 0))
    rowspec = pl.BlockSpec((1, 1, H_ML, tm), lambda bi, si: (bi, si, 0, 0))
    cspec = pl.BlockSpec((1, H_ML, DH_ML, 2 * DH_ML), lambda bi, si: (bi, 0, 0, 0))
    mspec = pl.BlockSpec((1, H_ML, 1, 1), lambda bi, si: (bi, 0, 0, 0))
    return pl.pallas_call(
        functools.partial(_mlstm_kernel, chunk=chunk, n_chunks=n_chunks),
        out_shape=[jax.ShapeDtypeStruct((b, rows, ML_W), BF16), jax.ShapeDtypeStruct(c0.shape, F32),
                   jax.ShapeDtypeStruct(m0.shape, F32)],
        grid=(b, steps),
        in_specs=[tok, tok, tok, tok, col, col, rowspec, rowspec, cspec, mspec,
                  _resident(g)],
        out_specs=[tok, cspec, mspec],
        scratch_shapes=[pltpu.VMEM((H_ML, DH_ML, 2 * DH_ML), F32), pltpu.VMEM((H_ML, 1, 1), F32)],
        compiler_params=_cparams("parallel", "arbitrary"), name="mlstm")(q, k, v, o, icol, bcol, irow, brow, c0, m0, g)


def _mem_kv_kernel(mem_ref, g_ref, w_ref, gk_ref, k_ref, v_ref):
    h = (_rms(mem_ref[...]) * g_ref[...]).astype(BF16)
    kv = _dot(h, w_ref[...])
    k_ref[...] = _head_norm_lanes(kv[:, :MEM_W], DH_MEM) * gk_ref[...]
    v_ref[...] = kv[:, MEM_W:]


def _mem_kv(mem, g, w16, gk):
    rows, d = mem.shape
    tm = 256
    return pl.pallas_call(
        _mem_kv_kernel,
        out_shape=[jax.ShapeDtypeStruct((rows, MEM_W), F32)] * 2, grid=(rows // tm,),
        in_specs=[pl.BlockSpec((tm, d), lambda i: (i, 0)), _resident(g), _resident(w16), _resident(gk)],
        out_specs=[pl.BlockSpec((tm, MEM_W), lambda i: (i, 0))] * 2,
        compiler_params=_cparams("parallel"), name="mem_kv")(mem, g, w16, gk)


def _mem_attn_kernel(q_ref, k_ref, v_ref, o_ref):
    q = q_ref[0]
    k = k_ref[0].astype(BF16)
    v = v_ref[0].astype(BF16)
    cs = [slice(h * DH_MEM, (h + 1) * DH_MEM) for h in range(H_MEM)]
    s = [_dot_nt(q[:, c], k[:, c]) for c in cs]
    mx = [jnp.max(x, axis=-1, keepdims=True) for x in s]
    p = [jnp.exp(x - m) for x, m in zip(s, mx)]
    l = [jnp.sum(x, axis=-1, keepdims=True) for x in p]
    outs = [_dot(x.astype(BF16), v[:, c]) / d for x, c, d in zip(p, cs, l)]
    o_ref[0] = jnp.concatenate(outs, axis=-1).astype(BF16)


def _mem_attn(q, mk, mv, *, tm):
    b, rows, _ = q.shape
    nm = mk.shape[1]
    tok = pl.BlockSpec((1, tm, MEM_W), lambda bi, si: (bi, si, 0))
    mem = pl.BlockSpec((1, nm, MEM_W), lambda bi, si: (bi, 0, 0))
    return pl.pallas_call(
        _mem_attn_kernel, out_shape=jax.ShapeDtypeStruct(q.shape, BF16), grid=(b, rows // tm),
        in_specs=[tok, mem, mem], out_specs=tok,
        compiler_params=_cparams("parallel", "arbitrary"), name="mem_attn")(q, mk, mv)


def _merge_kernel(x_ref, fox_ref, ml_ref, mem_ref, gates_ref, wf_ref, wm_ref, wx_ref, wo_ref, y_ref):
    d = x_ref.shape[1]
    merged = (gates_ref[:, 0:d].astype(F32) * _dot(fox_ref[...], wf_ref[...])
              + gates_ref[:, d:2 * d].astype(F32) * _dot(ml_ref[...], wm_ref[...])
              + gates_ref[:, 2 * d:3 * d].astype(F32) * _dot(mem_ref[...], wx_ref[...]))
    y_ref[...] = x_ref[...] + _dot(merged.astype(BF16), wo_ref[...])


def _merge(x, fox, ml, mem, gates, lw, *, tm):
    t, d = x.shape
    rows = lambda i: (i, 0)
    ws = [lw['w_br_fox'], lw['w_br_ml'], lw['w_br_mem'], lw['w_out']]
    return pl.pallas_call(
        _merge_kernel, out_shape=jax.ShapeDtypeStruct((t, d), F32), grid=(t // tm,),
        in_specs=[pl.BlockSpec((tm, d), rows), pl.BlockSpec((tm, FOX_W), rows), pl.BlockSpec((tm, ML_W), rows),
                  pl.BlockSpec((tm, MEM_W), rows), pl.BlockSpec((tm, 3 * d), rows)]
                 + [_resident(w) for w in ws],
        out_specs=pl.BlockSpec((tm, d), rows),
        compiler_params=_cparams("parallel"), name="merge")(x, fox, ml, mem, gates, *ws)


def _ffn_kernel(x_ref, g_ref, wu_ref, cw_ref, cb_ref, wd_ref, p0_ref, p1_ref, y_ref, aux_ref, carry,
                *, seg, n_split):
    tm = x_ref.shape[-2]
    dff2 = wu_ref.shape[1]
    dff = dff2 // 2
    cw = dff // n_split
    x = x_ref[0]
    h = (_rms(x) * g_ref[...]).astype(BF16)
    row = lax.broadcasted_iota(jnp.int32, (tm, 1), 0)
    if seg is None:
        @pl.when(pl.program_id(1) == 0)
        def _():
            carry[...] = p0_ref[0]
        first, second = row == 0, row == 1
    else:
        first, second = (row % seg) == 0, (row % seg) == 1

    def conv(cols):
        u = _dot(h, wu_ref[:, cols])
        if seg is None:
            hist0, hist1 = carry[0:1, cols], carry[1:2, cols]
            prev1 = jnp.where(first, hist1, pltpu.roll(u, 1, axis=0))
            prev2 = jnp.where(first, hist0, jnp.where(second, hist1, pltpu.roll(u, 2, axis=0)))
            carry[:, cols] = u[tm - 2:tm, :]
        else:
            prev1 = jnp.where(first, p1_ref[:, cols], pltpu.roll(u, 1, axis=0))
            prev2 = jnp.where(first, p0_ref[:, cols],
                              jnp.where(second, pltpu.roll(p1_ref[:, cols], 1, axis=0), pltpu.roll(u, 2, axis=0)))
            aux_ref[:, cols] = u
        return cb_ref[:, cols] + cw_ref[0:1, cols] * prev2 + cw_ref[1:2, cols] * prev1 + cw_ref[2:3, cols] * u

    acc = x
    for j in range(n_split):
        gate = conv(slice(j * cw, (j + 1) * cw))
        val = conv(slice(dff + j * cw, dff + (j + 1) * cw))
        act = (gate * _sigmoid(gate) * val).astype(BF16)
        acc = acc + _dot(act, wd_ref[j * cw:(j + 1) * cw, :])
    y_ref[0] = acc
    if seg is None:
        aux_ref[0] = carry[...]


def _ffn(x, lw, p0, p1, *, seg, tm):
    b, rows, d = x.shape
    dff2 = lw['w_up'].shape[1]
    tok = pl.BlockSpec((1, tm, d), lambda bi, si: (bi, si, 0))
    if seg is None:
        hist = pl.BlockSpec((1, CONV_W - 1, dff2), lambda bi, si: (bi, 0, 0))
        p_specs = [hist, hist]
        aux_shape, aux_spec = jax.ShapeDtypeStruct((b, CONV_W - 1, dff2), F32), hist
    else:
        full = pl.BlockSpec((tm, dff2), lambda bi, si: (si, 0))
        p_specs = [full, full]
        aux_shape, aux_spec = jax.ShapeDtypeStruct((rows, dff2), F32), full
    ws = [lw['norm2_g'], lw['w_up'], lw['conv_w'], lw['conv_b'], lw['w_down']]
    return pl.pallas_call(
        functools.partial(_ffn_kernel, seg=seg, n_split=2),
        out_shape=[jax.ShapeDtypeStruct(x.shape, F32), aux_shape], grid=(b, rows // tm),
        in_specs=[tok] + [_resident(w) for w in ws] + p_specs,
        out_specs=[tok, aux_spec],
        scratch_shapes=[pltpu.VMEM((CONV_W - 1, dff2), F32)],
        compiler_params=_cparams("parallel", "arbitrary"), name="ffn")(x, *ws, p0, p1)


def _in_offsets(d):
    sizes = (('fox_q', FOX_W), ('fox_k', FOX_W), ('fox_v', FOX_W), ('fox_f', H_FOX),
             ('ml_q', ML_W), ('ml_k', ML_W), ('ml_v', ML_W), ('ml_i', H_ML), ('ml_f', H_ML), ('ml_o', ML_W),
             ('mem_q', MEM_W), ('g_fox', d), ('g_ml', d), ('g_mem', d))
    offs, start = {}, 0
    for name, size in sizes:
        offs[name] = (start, size)
        start += size
    return offs


def _layer_weights(l, p):
    w_in, b_in = p['w_in'][l], p['b_in'][l]
    d = w_in.shape[0]
    offs = _in_offsets(d)

    def cols(names):
        w = jnp.concatenate([w_in[:, offs[n][0]:offs[n][0] + offs[n][1]] for n in names], axis=1)
        bias = jnp.concatenate([b_in[offs[n][0]:offs[n][0] + offs[n][1]] for n in names], axis=0)
        return w, bias

    normal = ['fox_k', 'fox_v', 'ml_q', 'ml_k', 'ml_v', 'ml_o', 'mem_q', 'g_fox', 'g_ml', 'g_mem']
    gates = ['fox_f', 'ml_i', 'ml_f']
    wn_p, bn_p = cols(normal)
    wn_s, bn_s = cols(normal + ['fox_q'])
    wt_p, bt_p = cols(gates + ['fox_q', 'fox_k', 'fox_v'])
    wt_s, bt_s = cols(gates)
    row = lambda a: a.reshape(1, -1)
    return dict(
        wn_p=wn_p.astype(BF16), bn_p=row(bn_p), wt_p=wt_p.T.astype(BF16), bt_p=bt_p.reshape(-1, 1),
        wn_s=wn_s.astype(BF16), bn_s=row(bn_s), wt_s=wt_s.T.astype(BF16), bt_s=bt_s.reshape(-1, 1),
        norm1_g=row(p['norm1_g'][l]), norm2_g=row(p['norm2_g'][l]),
        fox_knorm_g=row(jnp.tile(p['fox_knorm_g'][l], H_FOX)), fox_qnorm_g=row(jnp.tile(p['fox_qnorm_g'][l], H_FOX)),
        fox_qnorm_gc=p['fox_qnorm_g'][l].reshape(-1, 1), fox_knorm_gc=p['fox_knorm_g'][l].reshape(-1, 1),
        mem_qnorm_g=row(jnp.tile(p['mem_qnorm_g'][l], H_MEM)), mem_knorm_g=row(jnp.tile(p['mem_knorm_g'][l], H_MEM)),
        mem_norm_g=row(p['mem_norm_g'][l]), w_mem_kv=p['w_mem_kv'][l].astype(BF16),
        ml_hnorm_g=p['ml_hnorm_g'][l],
        w_br_fox=p['w_br_fox'][l].astype(BF16), w_br_ml=p['w_br_ml'][l].astype(BF16),
        w_br_mem=p['w_br_mem'][l].astype(BF16), w_out=p['w_out'][l].astype(BF16),
        w_up=p['w_up'][l].astype(BF16), w_down=p['w_down'][l].astype(BF16),
        conv_w=p['conv_w'][l], conv_b=row(p['conv_b'][l]))


def _gate_layouts(ct, b, steps, tm):
    i_row = ct[H_FOX:H_FOX + H_ML].reshape(H_ML, b, steps, tm).transpose(1, 2, 0, 3)
    b_row = ct[H_FOX + H_ML:].reshape(H_ML, b, steps, tm).transpose(1, 2, 0, 3)
    return i_row, b_row, i_row.transpose(0, 1, 3, 2), b_row.transpose(0, 1, 3, 2)


def _state_aug(c, n):
    return jnp.concatenate([c, n[..., None], jnp.zeros(c.shape[:-1] + (DH_ML - 1,), F32)], axis=-1)


def _layer_prompt(x, mem, lw):
    b, s, d = x.shape
    t = b * s
    tm = min(ROW_TILE, s)
    (fk16, mq, mk, mv, mo, memq, gates, gt, qT, vT, kT32, vT32) = _in_proj(
        x.reshape(t, d), lw, transposed_qv=True, tm=tm, seq=s)
    ct = _scan(gt, width=s, seg_fox=s, seg_ml=min(LANES, s), valid=None)
    n_hp = H_FOX // 2
    fcum = ct[:H_FOX]
    crow = fcum.reshape(n_hp, 2, t // tm, tm).transpose(2, 0, 1, 3)
    kc_chunk = min(2 * LANES, s)
    ckey = fcum.reshape(n_hp, 2, b, s // kc_chunk, kc_chunk).transpose(2, 0, 3, 1, 4)
    fox = _fox_prompt(qT, fk16.reshape(b, s, FOX_W), vT.reshape(b, s // tm, FOX_W, tm), crow, ckey,
                      b=b, s=s, tq=tm, tk=tm)
    chunk = min(LANES, s)
    n_chunks = max(1, min(4, s // chunk))
    m_tm = chunk * n_chunks
    i_row, b_row, i_col, b_col = _gate_layouts(ct, b, s // m_tm, m_tm)
    r3 = lambda a: a.reshape(b, s, -1)
    c0 = jnp.zeros((b, H_ML, DH_ML, 2 * DH_ML), F32)
    m0 = jnp.zeros((b, H_ML, 1, 1), F32)
    ml, c_new, m_new = _mlstm(r3(mq), r3(mk), r3(mv), r3(mo), i_col, b_col, i_row, b_row, c0, m0,
                              lw['ml_hnorm_g'], chunk=chunk, n_chunks=n_chunks)
    nm = mem.shape[1]
    mk32, mv32 = _mem_kv(mem.reshape(b * nm, d), lw['mem_norm_g'], lw['w_mem_kv'], lw['mem_knorm_g'])
    mk32, mv32 = mk32.reshape(b, nm, MEM_W), mv32.reshape(b, nm, MEM_W)
    mem_out = _mem_attn(r3(memq), mk32, mv32, tm=tm)
    x1 = _merge(x.reshape(t, d), fox.reshape(t, FOX_W), ml.reshape(t, ML_W), mem_out.reshape(t, MEM_W), gates,
                lw, tm=tm)
    dff2 = lw['w_up'].shape[1]
    hist = jnp.zeros((b, CONV_W - 1, dff2), F32)
    y, conv_new = _ffn(x1.reshape(b, s, d), lw, hist, hist, seg=None, tm=tm)
    state = dict(
        fk=kT32.reshape(b, H_FOX, DH_FOX, s).transpose(0, 3, 1, 2),
        fv=vT32.reshape(b, H_FOX, DH_FOX, s).transpose(0, 3, 1, 2),
        fl=fcum_to_logf(gt, b, s),
        mc=c_new[..., :DH_ML], mn=c_new[..., DH_ML], mm=m_new.reshape(b, H_ML),
        mk=mk32.reshape(b, nm, H_MEM, DH_MEM), mv=mv32.reshape(b, nm, H_MEM, DH_MEM), cv=conv_new)
    return y, state


def fcum_to_logf(gt, b, s):
    return gt[:H_FOX].reshape(H_FOX, b, s).transpose(1, 2, 0)


def _layer_sample(layer, x, lw, page_table, kc, vc, lc, mem_k, mem_v, c_st, n_st, m_st, conv_st, n_valid):
    db, tp, d = x.shape
    t = db * tp
    (fk16, mq, mk, mv, mo, memq, gates, gt, fq, fk32, fv32) = _in_proj(
        x.reshape(t, d), lw, transposed_qv=False, tm=min(ROW_TILE, t))
    ct = _scan(gt, width=t, seg_fox=tp, seg_ml=tp, valid=n_valid)
    r3 = lambda a: a.reshape(db, tp, -1)
    fcum = ct[:H_FOX].reshape(H_FOX, db, tp)
    crow = fcum.transpose(1, 0, 2)
    fox = _fox_sample(layer, n_valid, page_table, r3(fq), r3(fk16), r3(fv32), crow,
                      crow.reshape(db, H_FOX * tp, 1), kc, vc, lc)
    i_row, b_row, i_col, b_col = _gate_layouts(ct, db, 1, tp)
    ml, c_new, m_new = _mlstm(r3(mq), r3(mk), r3(mv), r3(mo), i_col, b_col, i_row, b_row,
                              _state_aug(c_st, n_st), m_st.reshape(db, H_ML, 1, 1), lw['ml_hnorm_g'],
                              chunk=tp, n_chunks=1)
    nm = mem_k.shape[1]
    mem_out = _mem_attn(r3(memq), mem_k.reshape(db, nm, MEM_W), mem_v.reshape(db, nm, MEM_W), tm=tp)
    tm = min(2 * ROW_TILE, t)
    x1 = _merge(x.reshape(t, d), fox.reshape(t, FOX_W), ml.reshape(t, ML_W), mem_out.reshape(t, MEM_W), gates,
                lw, tm=tm)
    pad_rows = lambda a: jnp.pad(a, ((0, 0), (0, tp - 1), (0, 0))).reshape(t, -1)
    p0, p1 = pad_rows(conv_st[:, 0:1]), pad_rows(conv_st[:, 1:2])
    y, u = _ffn(x1.reshape(1, t, d), lw, p0, p1, seg=tp, tm=min(LANES, t))
    v4 = lambda a: a.reshape((db, tp) + a.shape[1:])[:, :n_valid]
    state = dict(
        fk=v4(fk32).reshape(db, n_valid, H_FOX, DH_FOX), fv=v4(fv32).reshape(db, n_valid, H_FOX, DH_FOX),
        fl=fcum_to_logf(gt, db, tp)[:, :n_valid],
        mc=c_new[..., :DH_ML], mn=c_new[..., DH_ML], mm=m_new.reshape(db, H_ML),
        cv=v4(u)[:, n_valid - (CONV_W - 1):])
    return y.reshape(db, tp, d), state


def kernel(x_prompt, x_sample, mem_prompt, cache_fox_k, cache_fox_v, cache_fox_logf, cache_mem_k, cache_mem_v,
           state_mlstm_C, state_mlstm_n, state_mlstm_m, state_ffn_conv, page_table, norm1_g, w_in, b_in,
           fox_qnorm_g, fox_knorm_g, ml_hnorm_g, mem_norm_g, w_mem_kv, mem_qnorm_g, mem_knorm_g, w_br_fox,
           w_br_ml, w_br_mem, w_out, norm2_g, w_up, conv_w, conv_b, w_down):
    params = dict(norm1_g=norm1_g, w_in=w_in, b_in=b_in, fox_qnorm_g=fox_qnorm_g, fox_knorm_g=fox_knorm_g,
                  ml_hnorm_g=ml_hnorm_g, mem_norm_g=mem_norm_g, w_mem_kv=w_mem_kv, mem_qnorm_g=mem_qnorm_g,
                  mem_knorm_g=mem_knorm_g, w_br_fox=w_br_fox, w_br_ml=w_br_ml, w_br_mem=w_br_mem, w_out=w_out,
                  norm2_g=norm2_g, w_up=w_up, conv_w=conv_w, conv_b=conv_b, w_down=w_down)
    depth, n_phys, page = cache_fox_logf.shape[:3]
    db, n_valid, d = x_sample.shape
    kc = jnp.transpose(cache_fox_k, (0, 1, 3, 4, 2))
    vc = jnp.transpose(cache_fox_v, (0, 1, 3, 4, 2))
    lf_t = jnp.swapaxes(cache_fox_logf, 2, 3).reshape(depth * n_phys * H_FOX, page)
    lc = _page_scan(lf_t).reshape(depth, n_phys, H_FOX, page)
    yp = x_prompt
    ys = jnp.pad(x_sample, ((0, 0), (0, SAMPLE_PAD - n_valid), (0, 0)))
    sp, ss = [], []
    for l in range(depth):
        lw = _layer_weights(l, params)
        yp, st = _layer_prompt(yp, mem_prompt, lw)
        sp.append(st)
        ys, st = _layer_sample(l, ys, lw, page_table, kc, vc, lc, cache_mem_k[l], cache_mem_v[l],
                               state_mlstm_C[l], state_mlstm_n[l], state_mlstm_m[l], state_ffn_conv[l], n_valid)
        ss.append(st)
    stack = lambda sts, key: jnp.stack([s[key] for s in sts])
    return (yp, ys[:, :n_valid],
            stack(sp, 'fk'), stack(sp, 'fv'), stack(sp, 'fl'), stack(ss, 'fk'), stack(ss, 'fv'), stack(ss, 'fl'),
            stack(sp, 'mc'), stack(sp, 'mn'), stack(sp, 'mm'), stack(ss, 'mc'), stack(ss, 'mn'), stack(ss, 'mm'),
            stack(sp, 'mk'), stack(sp, 'mv'), stack(sp, 'cv'), stack(ss, 'cv'))
```

```python
import functools

import jax
import jax.numpy as jnp
from jax import lax
from jax.experimental import pallas as pl
from jax.experimental.pallas import tpu as pltpu

F32 = jnp.float32
BF16 = jnp.bfloat16

H_FOX, DH_FOX = 8, 64
H_ML, DH_ML = 4, 128
H_MEM, DH_MEM = 4, 128
FOX_W = H_FOX * DH_FOX
ML_W = H_ML * DH_ML
MEM_W = H_MEM * DH_MEM
CONV_W = 3
EPS = 1e-6
NEG = -1e30
LOG2E = 1.4426950408889634
FOX_Q_SCALE = DH_FOX ** -0.5 * LOG2E

LANES = 128
V7X_VMEM_LIMIT = 56 * 1024 * 1024
SAMPLE_PAD = 16
ROW_TILE = 512
N_GATES = 16
PAGE_GROUP = 8
RING_GROUPS = 3


def _cparams(*sem):
    return pltpu.CompilerParams(dimension_semantics=sem, vmem_limit_bytes=V7X_VMEM_LIMIT)


def _resident(a):
    return pl.BlockSpec(a.shape, lambda *_: (0,) * a.ndim, pipeline_mode=pl.Buffered(1))


def _dot(a, b):
    return jnp.dot(a, b, preferred_element_type=F32)


def _dot_nt(a, b):
    return lax.dot_general(a, b, (((1,), (1,)), ((), ())), preferred_element_type=F32)


def _dot_tn(a, b):
    return lax.dot_general(a, b, (((0,), (0,)), ((), ())), preferred_element_type=F32)


def _split3(x):
    hi = x.astype(BF16).astype(F32)
    r = x - hi
    mid = r.astype(BF16).astype(F32)
    return [hi, mid, (r - mid).astype(BF16).astype(F32)]


def _log_sigmoid(x):
    return jnp.minimum(x, 0.0) - jnp.log1p(jnp.exp(-jnp.abs(x)))


def _sigmoid(x):
    return 1.0 / (1.0 + jnp.exp(-x))


def _rms(x):
    return x * lax.rsqrt(jnp.mean(x * x, axis=-1, keepdims=True) + EPS)


def _head_norm_lanes(z, dh):
    outs = []
    lane = lax.broadcasted_iota(jnp.int32, (1, LANES), 1)
    for j in range(z.shape[1] // LANES):
        x = z[:, j * LANES:(j + 1) * LANES]
        sq = x * x
        if dh == LANES:
            inv = lax.rsqrt(jnp.sum(sq, axis=-1, keepdims=True) * (1.0 / dh) + EPS)
        else:
            lo = lane < dh
            s_lo = jnp.sum(jnp.where(lo, sq, 0.0), axis=-1, keepdims=True)
            s_hi = jnp.sum(jnp.where(lo, 0.0, sq), axis=-1, keepdims=True)
            inv = jnp.where(lo, lax.rsqrt(s_lo * (1.0 / dh) + EPS), lax.rsqrt(s_hi * (1.0 / dh) + EPS))
        outs.append(x * inv)
    return jnp.concatenate(outs, axis=-1)


def _in_proj_kernel(x_ref, g1_ref, wn_ref, bn_ref, wt_ref, bt_ref, gk_ref, gkc_ref, gq_ref, gqc_ref, gmq_ref,
                    *out_refs, transposed_qv):
    if transposed_qv:
        (fk16_ref, mq_ref, mk_ref, mv_ref, mo_ref, memq_ref, gates_ref,
         gt_ref, qT_ref, vT_ref, kT32_ref, vT32_ref) = out_refs
    else:
        (fk16_ref, mq_ref, mk_ref, mv_ref, mo_ref, memq_ref, gates_ref,
         gt_ref, fq_ref, fk32_ref, fv32_ref) = out_refs
    h = (_rms(x_ref[...]) * g1_ref[...]).astype(BF16)

    def slab(i, width=FOX_W):
        return _dot(h, wn_ref[:, i:i + width]) + bn_ref[:, i:i + width]

    fk = _head_norm_lanes(slab(0), DH_FOX) * gk_ref[...]
    fk16_ref[...] = fk.astype(BF16)
    if not transposed_qv:
        fk32_ref[...] = fk
        fv32_ref[...] = slab(FOX_W)
    mq_ref[...] = slab(2 * FOX_W).astype(BF16)
    mk_ref[...] = (slab(3 * FOX_W) * DH_ML ** -0.5).astype(BF16)
    mv_ref[...] = slab(4 * FOX_W).astype(BF16)
    mo_ref[...] = _sigmoid(slab(5 * FOX_W)).astype(BF16)
    memq_ref[...] = (_head_norm_lanes(slab(6 * FOX_W), DH_MEM) * gmq_ref[...] * DH_MEM ** -0.5).astype(BF16)
    g0 = 7 * FOX_W
    for j in range(3):
        gates_ref[:, j * 1024:(j + 1) * 1024] = _sigmoid(slab(g0 + j * 1024, 1024)).astype(BF16)

    zt = _dot_nt(wt_ref[...], h) + bt_ref[...]
    zg = zt[0:N_GATES]
    row = lax.broadcasted_iota(jnp.int32, (N_GATES, 1), 0)
    is_ml_i = (row >= H_FOX) & (row < H_FOX + H_ML)
    gt_ref[...] = jnp.where(is_ml_i, zg, _log_sigmoid(zg))
    if transposed_qv:
        tm = zt.shape[1]

        def head_norm_rows(z, g_col):
            z = z.reshape(H_FOX, DH_FOX, tm)
            inv = lax.rsqrt(jnp.sum(z * z, axis=1, keepdims=True) * (1.0 / DH_FOX) + EPS)
            return (z * inv * g_col.reshape(1, DH_FOX, 1)).reshape(FOX_W, tm)

        qT_ref[0] = (head_norm_rows(zt[N_GATES:N_GATES + FOX_W], gqc_ref[...]) * FOX_Q_SCALE).astype(BF16)
        kT32_ref[0] = head_norm_rows(zt[N_GATES + FOX_W:N_GATES + 2 * FOX_W], gkc_ref[...])
        vT = zt[N_GATES + 2 * FOX_W:N_GATES + 3 * FOX_W]
        vT32_ref[0] = vT
        vT_ref[0] = vT.astype(BF16)
    else:
        fq = _head_norm_lanes(slab(g0 + 3 * 1024), DH_FOX) * gq_ref[...] * FOX_Q_SCALE
        fq_ref[...] = fq.astype(BF16)


def _in_proj(x, lw, *, transposed_qv, tm, seq=None):
    t, d = x.shape
    wn, bn, wt, bt = (lw['wn_p'], lw['bn_p'], lw['wt_p'], lw['bt_p']) if transposed_qv else (
        lw['wn_s'], lw['bn_s'], lw['wt_s'], lw['bt_s'])
    nt = t // tm
    rows = lambda i: (i, 0)

    def row_out(width, dtype):
        return jax.ShapeDtypeStruct((t, width), dtype), pl.BlockSpec((tm, width), rows)

    outs = [row_out(FOX_W, BF16), row_out(ML_W, BF16),
            row_out(ML_W, BF16), row_out(ML_W, BF16), row_out(ML_W, BF16), row_out(MEM_W, BF16),
            row_out(3 * d, BF16),
            (jax.ShapeDtypeStruct((N_GATES, t), F32), pl.BlockSpec((N_GATES, tm), lambda i: (0, i)))]
    if transposed_qv:
        per_seq = seq // tm
        blk = (jax.ShapeDtypeStruct((nt, FOX_W, tm), BF16), pl.BlockSpec((1, FOX_W, tm), lambda i: (i, 0, 0)))
        seq_minor = (jax.ShapeDtypeStruct((t // seq, FOX_W, seq), F32),
                     pl.BlockSpec((1, FOX_W, tm), lambda i: (i // per_seq, 0, i % per_seq)))
        outs += [blk, blk, seq_minor, seq_minor]
    else:
        outs += [row_out(FOX_W, BF16), row_out(FOX_W, F32), row_out(FOX_W, F32)]
    ins = [x, lw['norm1_g'], wn, bn, wt, bt, lw['fox_knorm_g'], lw['fox_knorm_gc'], lw['fox_qnorm_g'],
           lw['fox_qnorm_gc'], lw['mem_qnorm_g']]
    in_specs = [pl.BlockSpec((tm, d), rows)] + [_resident(a) for a in ins[1:]]
    return pl.pallas_call(
        functools.partial(_in_proj_kernel, transposed_qv=transposed_qv),
        out_shape=[o[0] for o in outs], grid=(nt,), in_specs=in_specs, out_specs=[o[1] for o in outs],
        compiler_params=_cparams("parallel"), name="in_proj")(*ins)


def _scan_kernel(g_ref, o_ref, *, seg_fox, seg_ml, valid):
    x = g_ref[...]
    w = x.shape[1]
    row = lax.broadcasted_iota(jnp.int32, (N_GATES, 1), 0)
    pos = lax.broadcasted_iota(jnp.int32, (1, w), 1)
    is_fox = row < H_FOX
    is_ml_i = (row >= H_FOX) & (row < H_FOX + H_ML)
    if valid is not None:
        pad = (pos % seg_ml) >= valid
        x = jnp.where(pad & is_ml_i, NEG, jnp.where(pad & jnp.logical_not(is_fox), 0.0, x))
    segpos = jnp.where(is_fox, pos % seg_fox, pos % seg_ml)
    sh = 1
    while sh < max(seg_fox, seg_ml):
        take = (segpos >= sh) & jnp.logical_not(is_ml_i)
        x = x + jnp.where(take, pltpu.roll(x, sh, axis=1), 0.0)
        sh *= 2
    o_ref[...] = x


def _scan(gt, *, width, seg_fox, seg_ml, valid):
    t = gt.shape[1]
    return pl.pallas_call(
        functools.partial(_scan_kernel, seg_fox=seg_fox, seg_ml=seg_ml, valid=valid),
        out_shape=jax.ShapeDtypeStruct(gt.shape, F32), grid=(t // width,),
        in_specs=[pl.BlockSpec((N_GATES, width), lambda i: (0, i))],
        out_specs=pl.BlockSpec((N_GATES, width), lambda i: (0, i)),
        compiler_params=_cparams("parallel"), name="gate_scan")(gt)


def _page_scan_kernel(x_ref, o_ref):
    x = x_ref[...]
    pos = lax.broadcasted_iota(jnp.int32, (1, x.shape[1]), 1)
    sh = 1
    while sh < x.shape[1]:
        x = x + jnp.where(pos >= sh, pltpu.roll(x, sh, axis=1), 0.0)
        sh *= 2
    o_ref[...] = x


def _page_scan(lf_t):
    rows, page = lf_t.shape
    tr = min(4096, rows)
    return pl.pallas_call(
        _page_scan_kernel, out_shape=jax.ShapeDtypeStruct(lf_t.shape, F32), grid=(rows // tr,),
        in_specs=[pl.BlockSpec((tr, page), lambda i: (i, 0))],
        out_specs=pl.BlockSpec((tr, page), lambda i: (i, 0)),
        compiler_params=_cparams("parallel"), name="page_scan")(lf_t)


def _fox_prompt_kernel(qT_ref, k_ref, vT_ref, crow_ref, ckey_ref, o_ref, aug_scr, rhs_scr, s_a, s_b, mx_a, mx_b,
                       p_scr, m_scr, l_scr, al_scr, acc_scr, *, tq, tk):
    assert tq == tk
    qi = pl.program_id(2)
    n_feat = 2 * DH_FOX

    @pl.when(qi == 0)
    def _():
        n_chunk, _, tr = ckey_ref.shape[2:]
        feat_row = lax.broadcasted_iota(jnp.int32, (n_feat, 1), 0)

        def fill(i, carry):
            c = ckey_ref[0, 0, i] * (-LOG2E)
            feats = [jnp.ones((1, tr), F32)] * 3 + _split3(c[0:1, :]) + _split3(c[1:2, :])
            a = jnp.zeros((n_feat, tr), F32)
            for j, f in enumerate(feats):
                a = jnp.where(feat_row == j, f, a)
            aug_scr[pl.ds(pl.multiple_of(i * tr, tr), tr), :] = a.T.astype(BF16)
            return carry

        lax.fori_loop(0, n_chunk, fill, 0)

    qT2 = qT_ref[0]
    rows = lax.broadcasted_iota(jnp.int32, (n_feat, 1), 0)
    cq = crow_ref[0, 0] * LOG2E
    for hh in range(2):
        q_own = jnp.where((rows >= hh * DH_FOX) & (rows < (hh + 1) * DH_FOX), qT2, jnp.zeros_like(qT2))
        hi, mid, lo = _split3(cq[hh:hh + 1, :])
        ones_rows = (rows >= 3 + 3 * hh) & (rows < 6 + 3 * hh)
        feat = jnp.where(rows == 0, hi, jnp.where(rows == 1, mid, jnp.where(rows == 2, lo,
                         jnp.where(ones_rows, 1.0, 0.0))))
        rhs_scr[hh] = jnp.concatenate([q_own, feat.astype(BF16)], axis=0)
    m_scr[...] = jnp.full(m_scr.shape, NEG, F32)
    l_scr[...] = jnp.zeros(l_scr.shape, F32)
    al_scr[...] = jnp.ones(al_scr.shape, F32)
    acc_scr[...] = jnp.zeros(acc_scr.shape, F32)
    p_scr[...] = jnp.zeros(p_scr.shape, BF16)

    n_col = tq // LANES

    def put_cols(ref, hh, x):
        for c in range(n_col):
            ref[hh, c] = x[:, c * LANES:(c + 1) * LANES]

    def get_cols(ref, hh):
        return jnp.concatenate([ref[hh, c] for c in range(n_col)], axis=1)

    def scores(kb, s_ref, mx_ref):
        k0 = pl.multiple_of(kb * tk, tk)
        lhs = jnp.concatenate([k_ref[0, pl.ds(k0, tk), :], aug_scr[pl.ds(k0, tk), :]], axis=1)
        for hh in range(2):
            s = _dot(lhs, rhs_scr[hh])
            put_cols(s_ref, hh, s)
            mx_ref[hh] = jnp.max(s, axis=0, keepdims=True)

    def values(kb):
        vT2 = vT_ref[0, kb]
        for hh in range(2):
            acc_scr[hh] = (al_scr[hh] * acc_scr[hh]
                           + _dot(vT2[hh * DH_FOX:(hh + 1) * DH_FOX, :], get_cols(p_scr, hh)))

    def softmax(hh, s, mx):
        m_old = m_scr[hh]
        m_new = jnp.maximum(m_old, mx)
        alpha = jnp.exp2(m_old - m_new)
        p = jnp.exp2(s - m_new)
        l_scr[hh] = alpha * l_scr[hh] + jnp.sum(p, axis=0, keepdims=True)
        put_cols(p_scr, hh, p.astype(BF16))
        m_scr[hh] = m_new
        al_scr[hh] = alpha

    def phase(j, s_cur, mx_cur, s_nxt, mx_nxt):
        values(jnp.maximum(j - 1, 0))
        scores(j + 1, s_nxt, mx_nxt)
        for hh in range(2):
            softmax(hh, get_cols(s_cur, hh), mx_cur[hh])

    scores(0, s_a, mx_a)

    def trip(t, carry):
        j = 2 * t
        phase(j, s_a, mx_a, s_b, mx_b)

        @pl.when(j + 1 < qi)
        def _():
            phase(j + 1, s_b, mx_b, s_a, mx_a)

        return carry

    lax.fori_loop(0, (qi + 1) // 2, trip, 0)

    def finish(s_ref):
        values(jnp.maximum(qi - 1, 0))
        half = tk // 2
        tri = (lax.broadcasted_iota(jnp.int32, (half, 1), 0) <= lax.broadcasted_iota(jnp.int32, (1, half), 1))
        colmax = lambda x: jnp.max(x, axis=0, keepdims=True)
        colsum = lambda x: jnp.sum(x, axis=0, keepdims=True)
        for hh in range(2):
            s = get_cols(s_ref, hh)
            s_tl = jnp.where(tri, s[:half, :half], -jnp.inf)
            s_tr = s[:half, half:]
            s_br = jnp.where(tri, s[half:, half:], -jnp.inf)
            mx = jnp.concatenate([colmax(s_tl), jnp.maximum(colmax(s_tr), colmax(s_br))], axis=1)
            m_old = m_scr[hh]
            m_new = jnp.maximum(m_old, mx)
            alpha = jnp.exp2(m_old - m_new)
            p_tl = jnp.exp2(s_tl - m_new[:, :half])
            p_tr = jnp.exp2(s_tr - m_new[:, half:])
            p_br = jnp.exp2(s_br - m_new[:, half:])
            l_scr[hh] = alpha * l_scr[hh] + jnp.concatenate([colsum(p_tl), colsum(p_tr) + colsum(p_br)], axis=1)
            top = jnp.concatenate([p_tl, p_tr], axis=1)
            bottom = jnp.concatenate([jnp.zeros((half, half), F32), p_br], axis=1)
            put_cols(p_scr, hh, jnp.concatenate([top, bottom], axis=0).astype(BF16))
            m_scr[hh] = m_new
            al_scr[hh] = alpha
        values(qi)
        outT = jnp.concatenate([acc_scr[0] / l_scr[0], acc_scr[1] / l_scr[1]], axis=0)
        o_ref[0] = outT.T.astype(BF16)

    @pl.when(qi % 2 == 0)
    def _():
        finish(s_a)

    @pl.when(qi % 2 == 1)
    def _():
        finish(s_b)


def _fox_prompt(qT, k16, vT, crow, ckey, *, b, s, tq, tk):
    nq, nk = s // tq, s // tk
    n_hp = H_FOX // 2
    pair = lambda shape, dtype: pltpu.VMEM((2,) + shape, dtype)
    slabs = (tq // LANES, tk, LANES)
    return pl.pallas_call(
        functools.partial(_fox_prompt_kernel, tq=tq, tk=tk),
        scratch_shapes=[pltpu.VMEM((s, 2 * DH_FOX), BF16), pair((4 * DH_FOX, tq), BF16),
                        pair(slabs, F32), pair(slabs, F32), pair((1, tq), F32), pair((1, tq), F32),
                        pair(slabs, BF16), pair((1, tq), F32), pair((1, tq), F32), pair((1, tq), F32),
                        pair((DH_FOX, tq), F32)],
        out_shape=jax.ShapeDtypeStruct((b, s, FOX_W), BF16), grid=(b, n_hp, nq),
        in_specs=[
            pl.BlockSpec((1, 2 * DH_FOX, tq), lambda bi, hp, qi: (bi * nq + qi, hp, 0)),
            pl.BlockSpec((1, s, 2 * DH_FOX), lambda bi, hp, qi: (bi, 0, hp)),
            pl.BlockSpec((1, nk, 2 * DH_FOX, tk), lambda bi, hp, qi: (bi, 0, hp, 0)),
            pl.BlockSpec((1, 1, 2, tq), lambda bi, hp, qi: (bi * nq + qi, hp, 0, 0)),
            pl.BlockSpec((1, 1) + ckey.shape[2:], lambda bi, hp, qi: (bi, hp, 0, 0, 0)),
        ],
        out_specs=pl.BlockSpec((1, tq, 2 * DH_FOX), lambda bi, hp, qi: (bi, qi, hp)),
        compiler_params=_cparams("parallel", "parallel", "arbitrary"), name="fox_prompt")(qT, k16, vT, crow, ckey)


def _fox_sample_kernel(pt_ref, q_ref, kn_ref, vn_ref, crow_ref, ccol_ref, kc_ref, vc_ref, lc_ref, o_ref,
                       kbuf, vbuf, lbuf, sem, m_scr, l_scr, acc_scr, *, layer, n_seq, n_pages, page, n_valid, group, ring):
    b = pl.program_id(0)
    nb = pl.num_programs(0)
    tp = q_ref.shape[1]
    rows = H_FOX * tp

    def rep_heads(a):
        return jnp.broadcast_to(a[:, None, :], (H_FOX, tp, a.shape[-1])).reshape(rows, a.shape[-1])

    n_groups = n_pages // group

    def group_copies(bi, gi, rslot):
        out = []
        for g in range(group):
            pid = pt_ref[bi, n_pages - 1 - (gi * group + g)]
            slot = rslot * group + g
            out += [pltpu.make_async_copy(kc_ref.at[layer, pid], kbuf.at[slot], sem.at[0, slot]),
                    pltpu.make_async_copy(vc_ref.at[layer, pid], vbuf.at[slot], sem.at[1, slot]),
                    pltpu.make_async_copy(lc_ref.at[layer, pid], lbuf.at[slot], sem.at[2, slot])]
        return out

    @pl.when(b == 0)
    def _():
        for d in range(min(ring - 1, n_seq * n_groups)):
            for c in group_copies(d // n_groups, d % n_groups, d):
                c.start()

    row_id = lax.broadcasted_iota(jnp.int32, (rows, 1), 0)
    own = (row_id // tp) == (lax.broadcasted_iota(jnp.int32, (1, FOX_W), 1) // DH_FOX)
    q = q_ref[0]
    q_bd = jnp.where(own, jnp.broadcast_to(q[None], (H_FOX, tp, FOX_W)).reshape(rows, FOX_W), jnp.zeros((), BF16))
    cc = ccol_ref[0] * LOG2E
    spos = lax.broadcasted_iota(jnp.int32, (1, tp), 1)
    new_ok = (spos <= (row_id % tp)) & (spos < n_valid)
    s = _dot_nt(q_bd, kn_ref[0]) + (cc - rep_heads(crow_ref[0] * LOG2E))
    s = jnp.where(new_ok, s, -jnp.inf)
    m = jnp.max(s, axis=1, keepdims=True)
    p = jnp.exp2(s - m)
    m_scr[...] = m
    l_scr[...] = jnp.sum(p, axis=1, keepdims=True)
    acc_scr[...] = _dot(p.astype(BF16), vn_ref[0].astype(BF16))

    def step(i, suffix):
        gidx = b * n_groups + i
        rslot = gidx % ring
        for c in group_copies(b, i, rslot):
            c.wait()
        nxt = gidx + (ring - 1)
        nxt_b = nxt // n_groups

        @pl.when(nxt_b < nb)
        def _():
            for c in group_copies(nxt_b, nxt % n_groups, nxt % ring):
                c.start()

        after, k_pages, v_pages = [], [], []
        for g in range(group):
            slot = rslot * group + g
            lcum = lbuf[slot]
            tot = lcum[:, page - 1:page]
            after.append(suffix + tot - lcum)
            suffix = suffix + tot
            k_pages.append(kbuf[slot].reshape(FOX_W, page).astype(BF16))
            v_pages.append(vbuf[slot].reshape(FOX_W, page).astype(BF16))
        bias = cc + rep_heads(jnp.concatenate(after, axis=1) * LOG2E)
        s = _dot(q_bd, jnp.concatenate(k_pages, axis=1)) + bias
        m_old = m_scr[...]
        m_new = jnp.maximum(m_old, jnp.max(s, axis=1, keepdims=True))
        alpha = jnp.exp2(m_old - m_new)
        p = jnp.exp2(s - m_new)
        m_scr[...] = m_new
        l_scr[...] = alpha * l_scr[...] + jnp.sum(p, axis=1, keepdims=True)
        acc_scr[...] = alpha * acc_scr[...] + _dot_nt(p.astype(BF16), jnp.concatenate(v_pages, axis=1))
        return suffix

    lax.fori_loop(0, n_groups, step, jnp.zeros((H_FOX, 1), F32))
    out = jnp.where(own, acc_scr[...] / l_scr[...], 0.0)
    o_ref[0] = sum(out[h * tp:(h + 1) * tp] for h in range(H_FOX)).astype(BF16)


def _fox_sample(layer, n_valid, page_table, q16, kn16, vn32, crow, ccol, kc, vc, lc):
    db, n_pages = page_table.shape
    tp = q16.shape[1]
    page = lc.shape[3]
    rows = H_FOX * tp
    group = max(g for g in range(1, PAGE_GROUP + 1) if n_pages % g == 0)
    slots = group * RING_GROUPS
    per_seq = lambda bi, pt: (bi, 0, 0)
    grid_spec = pltpu.PrefetchScalarGridSpec(
        num_scalar_prefetch=1, grid=(db,),
        in_specs=[pl.BlockSpec((1, tp, FOX_W), per_seq), pl.BlockSpec((1, tp, FOX_W), per_seq),
                  pl.BlockSpec((1, tp, FOX_W), per_seq), pl.BlockSpec((1, H_FOX, tp), per_seq),
                  pl.BlockSpec((1, rows, 1), per_seq),
                  pl.BlockSpec(memory_space=pl.ANY), pl.BlockSpec(memory_space=pl.ANY),
                  pl.BlockSpec(memory_space=pl.ANY)],
        out_specs=pl.BlockSpec((1, tp, FOX_W), per_seq),
        scratch_shapes=[pltpu.VMEM((slots, H_FOX, DH_FOX, page), F32), pltpu.VMEM((slots, H_FOX, DH_FOX, page), F32),
                        pltpu.VMEM((slots, H_FOX, page), F32), pltpu.SemaphoreType.DMA((3, slots)),
                        pltpu.VMEM((rows, 1), F32), pltpu.VMEM((rows, 1), F32), pltpu.VMEM((rows, FOX_W), F32)])
    return pl.pallas_call(
        functools.partial(_fox_sample_kernel, layer=layer, n_seq=db, n_pages=n_pages, page=page, n_valid=n_valid,
                          group=group, ring=RING_GROUPS),
        out_shape=jax.ShapeDtypeStruct((db, tp, FOX_W), BF16), grid_spec=grid_spec,
        compiler_params=_cparams("arbitrary"), name="fox_sample")(page_table, q16, kn16, vn32, crow, ccol, kc, vc, lc)


def _mlstm_kernel(q_ref, k_ref, v_ref, o_ref, icol_ref, bcol_ref, irow_ref, brow_ref, c0_ref, m0_ref, g_ref,
                  out_ref, cn_ref, mn_ref, c_scr, m_scr, *, chunk, n_chunks):
    st = pl.program_id(1)

    @pl.when(st == 0)
    def _():
        c_scr[...] = c0_ref[0]
        m_scr[...] = m0_ref[0]

    lane = lax.broadcasted_iota(jnp.int32, (chunk, DH_ML), 1)
    ones_col = jnp.where(lane == 0, 1.0, 0.0).astype(BF16)
    tpos = lax.broadcasted_iota(jnp.int32, (chunk, 1), 0)
    spos = lax.broadcasted_iota(jnp.int32, (1, chunk), 1)
    causal = tpos >= spos
    heads = range(H_ML)
    cs = [slice(h * DH_ML, (h + 1) * DH_ML) for h in heads]
    each = lambda f: [f(h) for h in heads]
    pre = []
    for c in range(n_chunks):
        rs = slice(c * chunk, (c + 1) * chunk)
        bc = each(lambda h: bcol_ref[0, 0, rs, h:h + 1])
        dlog = each(lambda h: jnp.where(causal, bc[h] - brow_ref[0, 0, h:h + 1, rs] + irow_ref[0, 0, h:h + 1, rs],
                                        -jnp.inf))
        dmax = each(lambda h: jnp.max(dlog[h], axis=1, keepdims=True))
        qk = each(lambda h: _dot_nt(q_ref[0, rs, cs[h]], k_ref[0, rs, cs[h]]))
        pre.append((bc, dlog, dmax, qk))
    m_st = each(lambda h: m_scr[h])
    c_aug = each(lambda h: c_scr[h])
    for c in range(n_chunks):
        rs = slice(c * chunk, (c + 1) * chunk)
        bc, dlog, dmax, qk = pre[c]
        inter = each(lambda h: bc[h] + m_st[h])
        m_t = each(lambda h: jnp.maximum(inter[h], dmax[h]))
        w_intra = each(lambda h: jnp.exp(dlog[h] - m_t[h]))
        w_inter = each(lambda h: jnp.exp(inter[h] - m_t[h]))
        sc = each(lambda h: (qk[h] * w_intra[h]).astype(BF16))
        v_aug = each(lambda h: jnp.concatenate([v_ref[0, rs, cs[h]], ones_col], axis=1))
        numden = each(lambda h: _dot(sc[h], v_aug[h])
                      + w_inter[h] * _dot(q_ref[0, rs, cs[h]], c_aug[h].astype(BF16)))
        hid = each(lambda h: numden[h][:, :DH_ML]
                   / jnp.maximum(jnp.abs(numden[h][:, DH_ML:DH_ML + 1]), jnp.exp(-m_t[h])))
        m_new = each(lambda h: m_t[h][chunk - 1:chunk, :])
        b_last = each(lambda h: bc[h][chunk - 1:chunk, :])
        w_last = each(lambda h: jnp.exp(b_last[h] - bc[h] + icol_ref[0, 0, rs, h:h + 1] - m_new[h]))
        decay = each(lambda h: jnp.exp(b_last[h] + m_st[h] - m_new[h]))
        kw = each(lambda h: (k_ref[0, rs, cs[h]].astype(F32) * w_last[h]).astype(BF16))
        c_aug = each(lambda h: decay[h] * c_aug[h] + _dot_tn(kw[h], v_aug[h]))
        m_st = m_new
        normed = each(lambda h: _rms(hid[h]) * g_ref[h:h + 1, :])
        for h in heads:
            out_ref[0, rs, cs[h]] = (normed[h] * o_ref[0, rs, cs[h]].astype(F32)).astype(BF16)
    for h in heads:
        c_scr[h] = c_aug[h]
        m_scr[h] = m_st[h]
    cn_ref[0] = c_scr[...]
    mn_ref[0] = m_scr[...]


def _mlstm(q, k, v, o, icol, bcol, irow, brow, c0, m0, g, *, chunk, n_chunks):
    b, rows, _ = q.shape
    tm = chunk * n_chunks
    steps = rows // tm
    tok = pl.BlockSpec((1, tm, ML_W), lambda bi, si: (bi, si, 0))
    col = pl.BlockSpec((1, 1, tm, H_ML), lambda bi, si: (bi, si, 0, 0))
    rowspec = pl.BlockSpec((1, 1, H_ML, tm), lambda bi, si: (bi, si, 0, 0))
    cspec = pl.BlockSpec((1, H_ML, DH_ML, 2 * DH_ML), lambda bi, si: (bi, 0, 0, 0))
    mspec = pl.BlockSpec((1, H_ML, 1, 1), lambda bi, si: (bi, 0, 0, 0))
    return pl.pallas_call(
        functools.partial(_mlstm_kernel, chunk=chunk, n_chunks=n_chunks),
        out_shape=[jax.ShapeDtypeStruct((b, rows, ML_W), BF16), jax.ShapeDtypeStruct(c0.shape, F32),
                   jax.ShapeDtypeStruct(m0.shape, F32)],
        grid=(b, steps),
        in_specs=[tok, tok, tok, tok, col, col, rowspec, rowspec, cspec, mspec,
                  _resident(g)],
        out_specs=[tok, cspec, mspec],
        scratch_shapes=[pltpu.VMEM((H_ML, DH_ML, 2 * DH_ML), F32), pltpu.VMEM((H_ML, 1, 1), F32)],
        compiler_params=_cparams("parallel", "arbitrary"), name="mlstm")(q, k, v, o, icol, bcol, irow, brow, c0, m0, g)


def _mem_kv_kernel(mem_ref, g_ref, w_ref, gk_ref, k_ref, v_ref):
    h = (_rms(mem_ref[...]) * g_ref[...]).astype(BF16)
    kv = _dot(h, w_ref[...])
    k_ref[...] = _head_norm_lanes(kv[:, :MEM_W], DH_MEM) * gk_ref[...]
    v_ref[...] = kv[:, MEM_W:]


def _mem_kv(mem, g, w16, gk):
    rows, d = mem.shape
    tm = 256
    return pl.pallas_call(
        _mem_kv_kernel,
        out_shape=[jax.ShapeDtypeStruct((rows, MEM_W), F32)] * 2, grid=(rows // tm,),
        in_specs=[pl.BlockSpec((tm, d), lambda i: (i, 0)), _resident(g), _resident(w16), _resident(gk)],
        out_specs=[pl.BlockSpec((tm, MEM_W), lambda i: (i, 0))] * 2,
        compiler_params=_cparams("parallel"), name="mem_kv")(mem, g, w16, gk)


def _mem_attn_kernel(q_ref, k_ref, v_ref, o_ref):
    q = q_ref[0]
    k = k_ref[0].astype(BF16)
    v = v_ref[0].astype(BF16)
    cs = [slice(h * DH_MEM, (h + 1) * DH_MEM) for h in range(H_MEM)]
    s = [_dot_nt(q[:, c], k[:, c]) for c in cs]
    mx = [jnp.max(x, axis=-1, keepdims=True) for x in s]
    p = [jnp.exp(x - m) for x, m in zip(s, mx)]
    l = [jnp.sum(x, axis=-1, keepdims=True) for x in p]
    outs = [_dot(x.astype(BF16), v[:, c]) / d for x, c, d in zip(p, cs, l)]
    o_ref[0] = jnp.concatenate(outs, axis=-1).astype(BF16)


def _mem_attn(q, mk, mv, *, tm):
    b, rows, _ = q.shape
    nm = mk.shape[1]
    tok = pl.BlockSpec((1, tm, MEM_W), lambda bi, si: (bi, si, 0))
    mem = pl.BlockSpec((1, nm, MEM_W), lambda bi, si: (bi, 0, 0))
    return pl.pallas_call(
        _mem_attn_kernel, out_shape=jax.ShapeDtypeStruct(q.shape, BF16), grid=(b, rows // tm),
        in_specs=[tok, mem, mem], out_specs=tok,
        compiler_params=_cparams("parallel", "arbitrary"), name="mem_attn")(q, mk, mv)


def _merge_kernel(x_ref, fox_ref, ml_ref, mem_ref, gates_ref, wf_ref, wm_ref, wx_ref, wo_ref, y_ref):
    d = x_ref.shape[1]
    merged = (gates_ref[:, 0:d].astype(F32) * _dot(fox_ref[...], wf_ref[...])
              + gates_ref[:, d:2 * d].astype(F32) * _dot(ml_ref[...], wm_ref[...])
              + gates_ref[:, 2 * d:3 * d].astype(F32) * _dot(mem_ref[...], wx_ref[...]))
    y_ref[...] = x_ref[...] + _dot(merged.astype(BF16), wo_ref[...])


def _merge(x, fox, ml, mem, gates, lw, *, tm):
    t, d = x.shape
    rows = lambda i: (i, 0)
    ws = [lw['w_br_fox'], lw['w_br_ml'], lw['w_br_mem'], lw['w_out']]
    return pl.pallas_call(
        _merge_kernel, out_shape=jax.ShapeDtypeStruct((t, d), F32), grid=(t // tm,),
        in_specs=[pl.BlockSpec((tm, d), rows), pl.BlockSpec((tm, FOX_W), rows), pl.BlockSpec((tm, ML_W), rows),
                  pl.BlockSpec((tm, MEM_W), rows), pl.BlockSpec((tm, 3 * d), rows)]
                 + [_resident(w) for w in ws],
        out_specs=pl.BlockSpec((tm, d), rows),
        compiler_params=_cparams("parallel"), name="merge")(x, fox, ml, mem, gates, *ws)


def _ffn_kernel(x_ref, g_ref, wu_ref, cw_ref, cb_ref, wd_ref, p0_ref, p1_ref, y_ref, aux_ref, carry,
                *, seg, n_split):
    tm = x_ref.shape[-2]
    dff2 = wu_ref.shape[1]
    dff = dff2 // 2
    cw = dff // n_split
    x = x_ref[0]
    h = (_rms(x) * g_ref[...]).astype(BF16)
    row = lax.broadcasted_iota(jnp.int32, (tm, 1), 0)
    if seg is None:
        @pl.when(pl.program_id(1) == 0)
        def _():
            carry[...] = p0_ref[0]
        first, second = row == 0, row == 1
    else:
        first, second = (row % seg) == 0, (row % seg) == 1

    def conv(cols):
        u = _dot(h, wu_ref[:, cols])
        if seg is None:
            hist0, hist1 = carry[0:1, cols], carry[1:2, cols]
            prev1 = jnp.where(first, hist1, pltpu.roll(u, 1, axis=0))
            prev2 = jnp.where(first, hist0, jnp.where(second, hist1, pltpu.roll(u, 2, axis=0)))
            carry[:, cols] = u[tm - 2:tm, :]
        else:
            prev1 = jnp.where(first, p1_ref[:, cols], pltpu.roll(u, 1, axis=0))
            prev2 = jnp.where(first, p0_ref[:, cols],
                              jnp.where(second, pltpu.roll(p1_ref[:, cols], 1, axis=0), pltpu.roll(u, 2, axis=0)))
            aux_ref[:, cols] = u
        return cb_ref[:, cols] + cw_ref[0:1, cols] * prev2 + cw_ref[1:2, cols] * prev1 + cw_ref[2:3, cols] * u

    acc = x
    for j in range(n_split):
        gate = conv(slice(j * cw, (j + 1) * cw))
        val = conv(slice(dff + j * cw, dff + (j + 1) * cw))
        act = (gate * _sigmoid(gate) * val).astype(BF16)
        acc = acc + _dot(act, wd_ref[j * cw:(j + 1) * cw, :])
    y_ref[0] = acc
    if seg is None:
        aux_ref[0] = carry[...]


def _ffn(x, lw, p0, p1, *, seg, tm):
    b, rows, d = x.shape
    dff2 = lw['w_up'].shape[1]
    tok = pl.BlockSpec((1, tm, d), lambda bi, si: (bi, si, 0))
    if seg is None:
        hist = pl.BlockSpec((1, CONV_W - 1, dff2), lambda bi, si: (bi, 0, 0))
        p_specs = [hist, hist]
        aux_shape, aux_spec = jax.ShapeDtypeStruct((b, CONV_W - 1, dff2), F32), hist
    else:
        full = pl.BlockSpec((tm, dff2), lambda bi, si: (si, 0))
        p_specs = [full, full]
        aux_shape, aux_spec = jax.ShapeDtypeStruct((rows, dff2), F32), full
    ws = [lw['norm2_g'], lw['w_up'], lw['conv_w'], lw['conv_b'], lw['w_down']]
    return pl.pallas_call(
        functools.partial(_ffn_kernel, seg=seg, n_split=2),
        out_shape=[jax.ShapeDtypeStruct(x.shape, F32), aux_shape], grid=(b, rows // tm),
        in_specs=[tok] + [_resident(w) for w in ws] + p_specs,
        out_specs=[tok, aux_spec],
        scratch_shapes=[pltpu.VMEM((CONV_W - 1, dff2), F32)],
        compiler_params=_cparams("parallel", "arbitrary"), name="ffn")(x, *ws, p0, p1)


def _in_offsets(d):
    sizes = (('fox_q', FOX_W), ('fox_k', FOX_W), ('fox_v', FOX_W), ('fox_f', H_FOX),
             ('ml_q', ML_W), ('ml_k', ML_W), ('ml_v', ML_W), ('ml_i', H_ML), ('ml_f', H_ML), ('ml_o', ML_W),
             ('mem_q', MEM_W), ('g_fox', d), ('g_ml', d), ('g_mem', d))
    offs, start = {}, 0
    for name, size in sizes:
        offs[name] = (start, size)
        start += size
    return offs


def _layer_weights(l, p):
    w_in, b_in = p['w_in'][l], p['b_in'][l]
    d = w_in.shape[0]
    offs = _in_offsets(d)

    def cols(names):
        w = jnp.concatenate([w_in[:, offs[n][0]:offs[n][0] + offs[n][1]] for n in names], axis=1)
        bias = jnp.concatenate([b_in[offs[n][0]:offs[n][0] + offs[n][1]] for n in names], axis=0)
        return w, bias

    normal = ['fox_k', 'fox_v', 'ml_q', 'ml_k', 'ml_v', 'ml_o', 'mem_q', 'g_fox', 'g_ml', 'g_mem']
    gates = ['fox_f', 'ml_i', 'ml_f']
    wn_p, bn_p = cols(normal)
    wn_s, bn_s = cols(normal + ['fox_q'])
    wt_p, bt_p = cols(gates + ['fox_q', 'fox_k', 'fox_v'])
    wt_s, bt_s = cols(gates)
    row = lambda a: a.reshape(1, -1)
    return dict(
        wn_p=wn_p.astype(BF16), bn_p=row(bn_p), wt_p=wt_p.T.astype(BF16), bt_p=bt_p.reshape(-1, 1),
        wn_s=wn_s.astype(BF16), bn_s=row(bn_s), wt_s=wt_s.T.astype(BF16), bt_s=bt_s.reshape(-1, 1),
        norm1_g=row(p['norm1_g'][l]), norm2_g=row(p['norm2_g'][l]),
        fox_knorm_g=row(jnp.tile(p['fox_knorm_g'][l], H_FOX)), fox_qnorm_g=row(jnp.tile(p['fox_qnorm_g'][l], H_FOX)),
        fox_qnorm_gc=p['fox_qnorm_g'][l].reshape(-1, 1), fox_knorm_gc=p['fox_knorm_g'][l].reshape(-1, 1),
        mem_qnorm_g=row(jnp.tile(p['mem_qnorm_g'][l], H_MEM)), mem_knorm_g=row(jnp.tile(p['mem_knorm_g'][l], H_MEM)),
        mem_norm_g=row(p['mem_norm_g'][l]), w_mem_kv=p['w_mem_kv'][l].astype(BF16),
        ml_hnorm_g=p['ml_hnorm_g'][l],
        w_br_fox=p['w_br_fox'][l].astype(BF16), w_br_ml=p['w_br_ml'][l].astype(BF16),
        w_br_mem=p['w_br_mem'][l].astype(BF16), w_out=p['w_out'][l].astype(BF16),
        w_up=p['w_up'][l].astype(BF16), w_down=p['w_down'][l].astype(BF16),
        conv_w=p['conv_w'][l], conv_b=row(p['conv_b'][l]))


def _gate_layouts(ct, b, steps, tm):
    i_row = ct[H_FOX:H_FOX + H_ML].reshape(H_ML, b, steps, tm).transpose(1, 2, 0, 3)
    b_row = ct[H_FOX + H_ML:].reshape(H_ML, b, steps, tm).transpose(1, 2, 0, 3)
    return i_row, b_row, i_row.transpose(0, 1, 3, 2), b_row.transpose(0, 1, 3, 2)


def _state_aug(c, n):
    return jnp.concatenate([c, n[..., None], jnp.zeros(c.shape[:-1] + (DH_ML - 1,), F32)], axis=-1)


def _layer_prompt(x, mem, lw):
    b, s, d = x.shape
    t = b * s
    tm = min(ROW_TILE, s)
    (fk16, mq, mk, mv, mo, memq, gates, gt, qT, vT, kT32, vT32) = _in_proj(
        x.reshape(t, d), lw, transposed_qv=True, tm=tm, seq=s)
    ct = _scan(gt, width=s, seg_fox=s, seg_ml=min(LANES, s), valid=None)
    n_hp = H_FOX // 2
    fcum = ct[:H_FOX]
    crow = fcum.reshape(n_hp, 2, t // tm, tm).transpose(2, 0, 1, 3)
    kc_chunk = min(2 * LANES, s)
    ckey = fcum.reshape(n_hp, 2, b, s // kc_chunk, kc_chunk).transpose(2, 0, 3, 1, 4)
    fox = _fox_prompt(qT, fk16.reshape(b, s, FOX_W), vT.reshape(b, s // tm, FOX_W, tm), crow, ckey,
                      b=b, s=s, tq=tm, tk=tm)
    chunk = min(LANES, s)
    n_chunks = max(1, min(4, s // chunk))
    m_tm = chunk * n_chunks
    i_row, b_row, i_col, b_col = _gate_layouts(ct, b, s // m_tm, m_tm)
    r3 = lambda a: a.reshape(b, s, -1)
    c0 = jnp.zeros((b, H_ML, DH_ML, 2 * DH_ML), F32)
    m0 = jnp.zeros((b, H_ML, 1, 1), F32)
    ml, c_new, m_new = _mlstm(r3(mq), r3(mk), r3(mv), r3(mo), i_col, b_col, i_row, b_row, c0, m0,
                              lw['ml_hnorm_g'], chunk=chunk, n_chunks=n_chunks)
    nm = mem.shape[1]
    mk32, mv32 = _mem_kv(mem.reshape(b * nm, d), lw['mem_norm_g'], lw['w_mem_kv'], lw['mem_knorm_g'])
    mk32, mv32 = mk32.reshape(b, nm, MEM_W), mv32.reshape(b, nm, MEM_W)
    mem_out = _mem_attn(r3(memq), mk32, mv32, tm=tm)
    x1 = _merge(x.reshape(t, d), fox.reshape(t, FOX_W), ml.reshape(t, ML_W), mem_out.reshape(t, MEM_W), gates,
                lw, tm=tm)
    dff2 = lw['w_up'].shape[1]
    hist = jnp.zeros((b, CONV_W - 1, dff2), F32)
    y, conv_new = _ffn(x1.reshape(b, s, d), lw, hist, hist, seg=None, tm=tm)
    state = dict(
        fk=kT32.reshape(b, H_FOX, DH_FOX, s).transpose(0, 3, 1, 2),
        fv=vT32.reshape(b, H_FOX, DH_FOX, s).transpose(0, 3, 1, 2),
        fl=fcum_to_logf(gt, b, s),
        mc=c_new[..., :DH_ML], mn=c_new[..., DH_ML], mm=m_new.reshape(b, H_ML),
        mk=mk32.reshape(b, nm, H_MEM, DH_MEM), mv=mv32.reshape(b, nm, H_MEM, DH_MEM), cv=conv_new)
    return y, state


def fcum_to_logf(gt, b, s):
    return gt[:H_FOX].reshape(H_FOX, b, s).transpose(1, 2, 0)


def _layer_sample(layer, x, lw, page_table, kc, vc, lc, mem_k, mem_v, c_st, n_st, m_st, conv_st, n_valid):
    db, tp, d = x.shape
    t = db * tp
    (fk16, mq, mk, mv, mo, memq, gates, gt, fq, fk32, fv32) = _in_proj(
        x.reshape(t, d), lw, transposed_qv=False, tm=min(ROW_TILE, t))
    ct = _scan(gt, width=t, seg_fox=tp, seg_ml=tp, valid=n_valid)
    r3 = lambda a: a.reshape(db, tp, -1)
    fcum = ct[:H_FOX].reshape(H_FOX, db, tp)
    crow = fcum.transpose(1, 0, 2)
    fox = _fox_sample(layer, n_valid, page_table, r3(fq), r3(fk16), r3(fv32), crow,
                      crow.reshape(db, H_FOX * tp, 1), kc, vc, lc)
    i_row, b_row, i_col, b_col = _gate_layouts(ct, db, 1, tp)
    ml, c_new, m_new = _mlstm(r3(mq), r3(mk), r3(mv), r3(mo), i_col, b_col, i_row, b_row,
                              _state_aug(c_st, n_st), m_st.reshape(db, H_ML, 1, 1), lw['ml_hnorm_g'],
                              chunk=tp, n_chunks=1)
    nm = mem_k.shape[1]
    mem_out = _mem_attn(r3(memq), mem_k.reshape(db, nm, MEM_W), mem_v.reshape(db, nm, MEM_W), tm=tp)
    tm = min(2 * ROW_TILE, t)
    x1 = _merge(x.reshape(t, d), fox.reshape(t, FOX_W), ml.reshape(t, ML_W), mem_out.reshape(t, MEM_W), gates,
                lw, tm=tm)
    pad_rows = lambda a: jnp.pad(a, ((0, 0), (0, tp - 1), (0, 0))).reshape(t, -1)
    p0, p1 = pad_rows(conv_st[:, 0:1]), pad_rows(conv_st[:, 1:2])
    y, u = _ffn(x1.reshape(1, t, d), lw, p0, p1, seg=tp, tm=min(LANES, t))
    v4 = lambda a: a.reshape((db, tp) + a.shape[1:])[:, :n_valid]
    state = dict(
        fk=v4(fk32).reshape(db, n_valid, H_FOX, DH_FOX), fv=v4(fv32).reshape(db, n_valid, H_FOX, DH_FOX),
        fl=fcum_to_logf(gt, db, tp)[:, :n_valid],
        mc=c_new[..., :DH_ML], mn=c_new[..., DH_ML], mm=m_new.reshape(db, H_ML),
        cv=v4(u)[:, n_valid - (CONV_W - 1):])
    return y.reshape(db, tp, d), state


def kernel(x_prompt, x_sample, mem_prompt, cache_fox_k, cache_fox_v, cache_fox_logf, cache_mem_k, cache_mem_v,
           state_mlstm_C, state_mlstm_n, state_mlstm_m, state_ffn_conv, page_table, norm1_g, w_in, b_in,
           fox_qnorm_g, fox_knorm_g, ml_hnorm_g, mem_norm_g, w_mem_kv, mem_qnorm_g, mem_knorm_g, w_br_fox,
           w_br_ml, w_br_mem, w_out, norm2_g, w_up, conv_w, conv_b, w_down):
    params = dict(norm1_g=norm1_g, w_in=w_in, b_in=b_in, fox_qnorm_g=fox_qnorm_g, fox_knorm_g=fox_knorm_g,
                  ml_hnorm_g=ml_hnorm_g, mem_norm_g=mem_norm_g, w_mem_kv=w_mem_kv, mem_qnorm_g=mem_qnorm_g,
                  mem_knorm_g=mem_knorm_g, w_br_fox=w_br_fox, w_br_ml=w_br_ml, w_br_mem=w_br_mem, w_out=w_out,
                  norm2_g=norm2_g, w_up=w_up, conv_w=conv_w, conv_b=conv_b, w_down=w_down)
    depth, n_phys, page = cache_fox_logf.shape[:3]
    db, n_valid, d = x_sample.shape
    kc = jnp.transpose(cache_fox_k, (0, 1, 3, 4, 2))
    vc = jnp.transpose(cache_fox_v, (0, 1, 3, 4, 2))
    lf_t = jnp.swapaxes(cache_fox_logf, 2, 3).reshape(depth * n_phys * H_FOX, page)
    lc = _page_scan(lf_t).reshape(depth, n_phys, H_FOX, page)
    yp = x_prompt
    ys = jnp.pad(x_sample, ((0, 0), (0, SAMPLE_PAD - n_valid), (0, 0)))
    sp, ss = [], []
    for l in range(depth):
        lw = _layer_weights(l, params)
        yp, st = _layer_prompt(yp, mem_prompt, lw)
        sp.append(st)
        ys, st = _layer_sample(l, ys, lw, page_table, kc, vc, lc, cache_mem_k[l], cache_mem_v[l],
                               state_mlstm_C[l], state_mlstm_n[l], state_mlstm_m[l], state_ffn_conv[l], n_valid)
        ss.append(st)
    stack = lambda sts, key: jnp.stack([s[key] for s in sts])
    return (yp, ys[:, :n_valid],
            stack(sp, 'fk'), stack(sp, 'fv'), stack(sp, 'fl'), stack(ss, 'fk'), stack(ss, 'fv'), stack(ss, 'fl'),
            stack(sp, 'mc'), stack(sp, 'mn'), stack(sp, 'mm'), stack(ss, 'mc'), stack(ss, 'mn'), stack(ss, 'mm'),
            stack(sp, 'mk'), stack(sp, 'mv'), stack(sp, 'cv'), stack(ss, 'cv'))
```
